```python
import jax, jax.numpy as jnp
from jax import lax
import numpy as np

D_MODEL = 1024
BATCH = 2
SEQ = 16384
DEPTH = 1
DEC_BATCH = 128
DEC_SEQ = 4
PAST_LEN = 8192
PAGE_SIZE = 128

N_FOX_HEADS = 8
FOX_HEAD_DIM = 64
Q_BLOCK = 128
N_GLA_HEADS = 4
GLA_DK = 64
GLA_DV = 128
GLA_GATE_RANK = 16
GLA_GATE_TEMP = 16.0
GLA_CHUNK = 64
PEER_HEADS = 8
PEER_N_KEYS = 128
PEER_TOPK = 16
PEER_QUERY_DIM = 128
PEER_BLOCK = 128
PEER_N_EXPERTS = PEER_N_KEYS * PEER_N_KEYS
PEER_HALF = PEER_QUERY_DIM // 2

RMS_EPS = 1e-6
NEG_INF = -1e30
FORGET_BIAS_LO = 2.0
FORGET_BIAS_HI = 10.0

FOX_WIDTH = N_FOX_HEADS * FOX_HEAD_DIM
GLA_QK_WIDTH = N_GLA_HEADS * GLA_DK
GLA_V_WIDTH = N_GLA_HEADS * GLA_DV
FOX_SCALE = FOX_HEAD_DIM ** -0.5
IN_SIZES = [FOX_WIDTH, FOX_WIDTH, FOX_WIDTH, N_FOX_HEADS,
            GLA_QK_WIDTH, GLA_QK_WIDTH, GLA_V_WIDTH, GLA_GATE_RANK, GLA_V_WIDTH]
IN_SPLITS = [int(c) for c in np.cumsum(IN_SIZES)[:-1]]
IN_COLS = int(sum(IN_SIZES))

kernel_name = 'hymba_fox_gla_peer_step'


def rmsnorm(x, g):
    x32 = x.astype(jnp.float32)
    y = x32 * lax.rsqrt(jnp.mean(x32 * x32, axis=-1, keepdims=True) + RMS_EPS)
    return y.astype(x.dtype) * g


def project(xn, w_in, b_forget, q_norm, k_norm, w_gate_up, b_gate):
    b, l, _ = xn.shape
    fq, fk, fv, ff, gq, gk, gv, ga, gr = jnp.split(xn @ w_in, IN_SPLITS, axis=-1)
    fq = rmsnorm(fq.reshape(b, l, N_FOX_HEADS, FOX_HEAD_DIM), q_norm)
    fk = rmsnorm(fk.reshape(b, l, N_FOX_HEADS, FOX_HEAD_DIM), k_norm)
    fv = fv.reshape(b, l, N_FOX_HEADS, FOX_HEAD_DIM)
    lf = jax.nn.log_sigmoid((ff + b_forget).astype(jnp.float32))
    gq = gq.reshape(b, l, N_GLA_HEADS, GLA_DK) * (GLA_DK ** -0.5)
    gk = gk.reshape(b, l, N_GLA_HEADS, GLA_DK)
    gv = gv.reshape(b, l, N_GLA_HEADS, GLA_DV)
    la = jax.nn.log_sigmoid((ga @ w_gate_up + b_gate).astype(jnp.float32)) / GLA_GATE_TEMP
    la = la.reshape(b, l, N_GLA_HEADS, GLA_DK)
    return fq, fk, fv, lf, gq, gk, gv, la, gr


def fox_prompt(q, k, v, lf):
    b, s, h, dh = q.shape
    nb = s // Q_BLOCK
    cum_t = jnp.cumsum(lf, axis=1).transpose(0, 2, 1)
    qb = q.reshape(b, nb, Q_BLOCK, h, dh).transpose(1, 0, 2, 3, 4)
    cb = cum_t.reshape(b, h, nb, Q_BLOCK).transpose(2, 0, 1, 3)
    pos_k = jnp.arange(s)

    def one(args):
        qi, ci, i = args
        logits = jnp.einsum('bthd,bshd->bhts', qi, k).astype(jnp.float32) * FOX_SCALE
        logits = logits + ci[..., :, None] - cum_t[..., None, :]
        pos_q = i * Q_BLOCK + jnp.arange(Q_BLOCK)
        logits = jnp.where(pos_k[None, :] <= pos_q[:, None], logits, NEG_INF)
        p = jax.nn.softmax(logits, axis=-1)
        return jnp.einsum('bhts,bshd->bthd', p.astype(v.dtype), v)

    o = lax.map(one, (qb, cb, jnp.arange(nb)))
    return o.transpose(1, 0, 2, 3, 4).reshape(b, s, h, dh)


def fox_sample(q, k_new, v_new, lf_new, cache_k, cache_v, cache_logf, layer, page_table):
    ds = q.shape[1]
    tril = jnp.tril(jnp.ones((ds, ds), dtype=bool))

    def one(args):
        qs, kn, vn, lfn, pages = args
        kp = cache_k[layer, pages].reshape(-1, N_FOX_HEADS, FOX_HEAD_DIM)
        vp = cache_v[layer, pages].reshape(-1, N_FOX_HEADS, FOX_HEAD_DIM)
        lfp = cache_logf[layer, pages].reshape(-1, N_FOX_HEADS).astype(jnp.float32)
        excl = jnp.flip(jnp.cumsum(jnp.flip(lfp, 0), 0), 0) - lfp
        cn = jnp.cumsum(lfn, 0).T
        lp = jnp.einsum('thd,shd->hts', qs, kp).astype(jnp.float32) * FOX_SCALE
        lp = lp + cn[:, :, None] + excl.T[:, None, :]
        ln = jnp.einsum('thd,shd->hts', qs, kn).astype(jnp.float32) * FOX_SCALE
        ln = jnp.where(tril, ln + cn[:, :, None] - cn[:, None, :], NEG_INF)
        n_past = lp.shape[-1]
        p = jax.nn.softmax(jnp.concatenate([lp, ln], axis=-1), axis=-1)
        return (jnp.einsum('hts,shd->thd', p[..., :n_past].astype(vp.dtype), vp)
                + jnp.einsum('hts,shd->thd', p[..., n_past:].astype(vn.dtype), vn))

    return lax.map(one, (q, k_new, v_new, lf_new, page_table))


def gla_chunked(q, k, v, la, s0, chunk):
    b, l, h, dk = q.shape
    dv = v.shape[-1]
    n = l // chunk

    def to_chunks(a):
        return a.astype(jnp.float32).reshape(b, n, chunk, h, a.shape[-1]).transpose(1, 0, 3, 2, 4)

    causal = jnp.tril(jnp.ones((chunk, chunk), dtype=bool))[None, None, :, :, None]

    def step(state, inp):
        qc, kc, vc, lc = inp
        cum = jnp.cumsum(lc, axis=2)
        diff = cum[:, :, :, None, :] - cum[:, :, None, :, :]
        decay = jnp.exp(jnp.where(causal, diff, -jnp.inf))
        scores = jnp.einsum('bhtd,bhsd,bhtsd->bhts', qc, kc, decay)
        out = (jnp.einsum('bhts,bhsv->bhtv', scores, vc)
               + jnp.einsum('bhtd,bhdv->bhtv', qc * jnp.exp(cum), state))
        last = cum[:, :, -1:, :]
        state = (jnp.exp(last[:, :, 0, :])[..., None] * state
                 + jnp.einsum('bhsd,bhsv->bhdv', kc * jnp.exp(last - cum), vc))
        return state, out

    state, out = lax.scan(step, s0.astype(jnp.float32),
                          (to_chunks(q), to_chunks(k), to_chunks(v), to_chunks(la)))
    out = out.transpose(1, 0, 3, 2, 4).reshape(b, l, h, dv)
    return out.astype(v.dtype), state


def merge(fox_o, gla_o, gr, gla_norm, w_out):
    b, l = fox_o.shape[:2]
    g = rmsnorm(gla_o, gla_norm) * jax.nn.silu(gr).reshape(b, l, N_GLA_HEADS, GLA_DV)
    cat = jnp.concatenate([fox_o.reshape(b, l, FOX_WIDTH),
                           g.astype(fox_o.dtype).reshape(b, l, GLA_V_WIDTH)], axis=-1)
    return cat @ w_out


def peer(xn, w_q, sub_keys, expert_u, expert_v):
    shape = xn.shape
    t = xn.reshape(-1, D_MODEL)
    n_tok = t.shape[0]
    pad = (-n_tok) % PEER_BLOCK
    blocks = jnp.pad(t, ((0, pad), (0, 0))).reshape(-1, PEER_BLOCK, D_MODEL)

    def one(xb):
        q = (xb @ w_q).reshape(PEER_BLOCK, PEER_HEADS, 2, PEER_HALF)
        s = jnp.einsum('thcd,hckd->thck', q, sub_keys).astype(jnp.float32)
        top_s, top_i = lax.top_k(s, PEER_TOPK)
        cand = (top_s[:, :, 0, :, None] + top_s[:, :, 1, None, :]).reshape(
            PEER_BLOCK, PEER_HEADS, PEER_TOPK * PEER_TOPK)
        score, sel = lax.top_k(cand, PEER_TOPK)
        i1 = jnp.take_along_axis(top_i[:, :, 0, :], sel // PEER_TOPK, axis=-1)
        i2 = jnp.take_along_axis(top_i[:, :, 1, :], sel % PEER_TOPK, axis=-1)
        idx = i1 * PEER_N_KEYS + i2
        gate = jax.nn.softmax(score, axis=-1)
        act = jax.nn.gelu(jnp.einsum('td,thkd->thk', xb, expert_u[idx]).astype(jnp.float32),
                          approximate=False)
        return jnp.einsum('thk,thkd->td', (gate * act).astype(xb.dtype), expert_v[idx])

    out = lax.map(one, blocks)
    return out.reshape(-1, D_MODEL)[:n_tok].reshape(shape)


def setup_inputs(seed: int = 0) -> dict:
    key = jax.random.key(seed)
    ks = jax.random.split(key, 24)
    f32 = jnp.float32
    n_pages = PAST_LEN // PAGE_SIZE
    n_used = DEC_BATCH * n_pages
    n_pool = n_used + n_used // 4
    forget_bias = jnp.linspace(FORGET_BIAS_LO, FORGET_BIAS_HI, N_FOX_HEADS, dtype=f32)

    def nrm(k, shape, scale):
        return scale * jax.random.normal(k, shape, f32)

    x_prompt = nrm(ks[0], (BATCH, SEQ, D_MODEL), 1.0)
    x_sample = nrm(ks[1], (DEC_BATCH, DEC_SEQ, D_MODEL), 1.0)
    cache_k = nrm(ks[2], (DEPTH, n_pool, PAGE_SIZE, N_FOX_HEADS, FOX_HEAD_DIM), 1.0)
    cache_v = nrm(ks[3], (DEPTH, n_pool, PAGE_SIZE, N_FOX_HEADS, FOX_HEAD_DIM), 1.0)
    cache_logf = jax.nn.log_sigmoid(forget_bias + nrm(ks[4], (DEPTH, n_pool, PAGE_SIZE, N_FOX_HEADS), 1.0))
    state_gla = nrm(ks[5], (DEPTH, DEC_BATCH, N_GLA_HEADS, GLA_DK, GLA_DV), 1.0)
    page_table = jax.random.permutation(ks[6], n_pool)[:n_used].reshape(DEC_BATCH, n_pages).astype(jnp.int32)

    norm_mix = 1.0 + nrm(ks[7], (DEPTH, D_MODEL), 0.05)
    w_in = nrm(ks[8], (DEPTH, D_MODEL, IN_COLS), D_MODEL ** -0.5)
    b_forget = forget_bias + nrm(ks[9], (DEPTH, N_FOX_HEADS), 0.1)
    q_norm = 1.0 + nrm(ks[10], (DEPTH, FOX_HEAD_DIM), 0.05)
    k_norm = 1.0 + nrm(ks[11], (DEPTH, FOX_HEAD_DIM), 0.05)
    w_gate_up = nrm(ks[12], (DEPTH, GLA_GATE_RANK, GLA_QK_WIDTH), GLA_GATE_RANK ** -0.5)
    b_gate = nrm(ks[13], (DEPTH, GLA_QK_WIDTH), 0.1)
    gla_norm = 1.0 + nrm(ks[14], (DEPTH, GLA_DV), 0.05)
    w_out = nrm(ks[15], (DEPTH, D_MODEL, D_MODEL), D_MODEL ** -0.5)
    norm_ffn = 1.0 + nrm(ks[16], (DEPTH, D_MODEL), 0.05)
    peer_wq = nrm(ks[17], (DEPTH, D_MODEL, PEER_HEADS * PEER_QUERY_DIM), D_MODEL ** -0.5)
    peer_sub_keys = nrm(ks[18], (DEPTH, PEER_HEADS, 2, PEER_N_KEYS, PEER_HALF), PEER_HALF ** -0.5)
    peer_u = nrm(ks[19], (DEPTH, PEER_N_EXPERTS, D_MODEL), D_MODEL ** -0.5)
    peer_v = nrm(ks[20], (DEPTH, PEER_N_EXPERTS, D_MODEL), PEER_HEADS ** -0.5)
    return {'x_prompt': x_prompt, 'x_sample': x_sample, 'cache_k': cache_k, 'cache_v': cache_v,
            'cache_logf': cache_logf, 'state_gla': state_gla, 'page_table': page_table,
            'norm_mix': norm_mix, 'w_in': w_in, 'b_forget': b_forget, 'q_norm': q_norm,
            'k_norm': k_norm, 'w_gate_up': w_gate_up, 'b_gate': b_gate, 'gla_norm': gla_norm,
            'w_out': w_out, 'norm_ffn': norm_ffn, 'peer_wq': peer_wq,
            'peer_sub_keys': peer_sub_keys, 'peer_u': peer_u, 'peer_v': peer_v}


def reference(x_prompt, x_sample, cache_k, cache_v, cache_logf, state_gla, page_table,
              norm_mix, w_in, b_forget, q_norm, k_norm, w_gate_up, b_gate, gla_norm,
              w_out, norm_ffn, peer_wq, peer_sub_keys, peer_u, peer_v):
    xp, xs = x_prompt, x_sample
    kp_l, vp_l, lfp_l, sp_l = [], [], [], []
    ks_l, vs_l, lfs_l, ss_l = [], [], [], []
    for l in range(DEPTH):
        fq, fk, fv, lf, gq, gk, gv, la, gr = project(rmsnorm(xp, norm_mix[l]), w_in[l], b_forget[l],
                                                     q_norm[l], k_norm[l], w_gate_up[l], b_gate[l])
        fo = fox_prompt(fq, fk, fv, lf)
        s0 = jnp.zeros((xp.shape[0], N_GLA_HEADS, GLA_DK, GLA_DV), jnp.float32)
        go, sp = gla_chunked(gq, gk, gv, la, s0, GLA_CHUNK)
        xp = xp + merge(fo, go, gr, gla_norm[l], w_out[l])
        xp = xp + peer(rmsnorm(xp, norm_ffn[l]), peer_wq[l], peer_sub_keys[l], peer_u[l], peer_v[l])
        kp_l.append(fk); vp_l.append(fv); lfp_l.append(lf); sp_l.append(sp)

        fq, fk, fv, lf, gq, gk, gv, la, gr = project(rmsnorm(xs, norm_mix[l]), w_in[l], b_forget[l],
                                                     q_norm[l], k_norm[l], w_gate_up[l], b_gate[l])
        fo = fox_sample(fq, fk, fv, lf, cache_k, cache_v, cache_logf, l, page_table)
        go, ss = gla_chunked(gq, gk, gv, la, state_gla[l], xs.shape[1])
        xs = xs + merge(fo, go, gr, gla_norm[l], w_out[l])
        xs = xs + peer(rmsnorm(xs, norm_ffn[l]), peer_wq[l], peer_sub_keys[l], peer_u[l], peer_v[l])
        ks_l.append(fk); vs_l.append(fv); lfs_l.append(lf); ss_l.append(ss)

    return (xp, xs,
            jnp.stack(kp_l), jnp.stack(vp_l), jnp.stack(lfp_l), jnp.stack(sp_l),
            jnp.stack(ks_l), jnp.stack(vs_l), jnp.stack(lfs_l), jnp.stack(ss_l))
```

```python
import functools

import numpy as np
import jax
import jax.numpy as jnp
from jax import lax
from jax.experimental import pallas as pl
from jax.experimental.pallas import tpu as pltpu

F32 = jnp.float32
BF16 = jnp.bfloat16
I32 = jnp.int32

D_MODEL = 1024
N_FOX_HEADS = 8
FOX_HEAD_DIM = 64
N_GLA_HEADS = 4
GLA_DK = 64
GLA_DV = 128
GLA_GATE_RANK = 16
GLA_GATE_TEMP = 16.0
GLA_CHUNK = 64
PEER_HEADS = 8
PEER_N_KEYS = 128
PEER_TOPK = 16
PEER_HALF = 64
PAGE_SIZE = 128
RMS_EPS = 1e-6
NEG_INF = -1e30

FOX_WIDTH = N_FOX_HEADS * FOX_HEAD_DIM
GLA_QK_WIDTH = N_GLA_HEADS * GLA_DK
GLA_V_WIDTH = N_GLA_HEADS * GLA_DV
FOX_SCALE = FOX_HEAD_DIM ** -0.5
GLA_SCALE = GLA_DK ** -0.5
IN_SIZES = [FOX_WIDTH, FOX_WIDTH, FOX_WIDTH, N_FOX_HEADS,
            GLA_QK_WIDTH, GLA_QK_WIDTH, GLA_V_WIDTH, GLA_GATE_RANK, GLA_V_WIDTH]
IN_OFFS = [int(c) for c in np.cumsum([0] + IN_SIZES)]
PEER_SLOTS = PEER_HEADS * PEER_TOPK

LANES = 128
SUBLANES = 8
VMEM_LIMIT = 56 * 1024 * 1024


def _cparams(sem, vmem=VMEM_LIMIT):
    return pltpu.CompilerParams(dimension_semantics=sem, vmem_limit_bytes=vmem)


def _dot(a, b):
    return jnp.dot(a, b, preferred_element_type=F32)


def _dot_nt(a, b):
    return lax.dot_general(a, b, (((1,), (1,)), ((), ())), preferred_element_type=F32)


def _split3(x):
    hi = x.astype(BF16)
    r = x - hi.astype(F32)
    mid = r.astype(BF16)
    lo = (r - mid.astype(F32)).astype(BF16)
    return hi, mid, lo


def _dot_sel_rhs(sel_bf16, x):
    hi, mid, lo = _split3(x)
    return _dot(sel_bf16, hi) + _dot(sel_bf16, mid) + _dot(sel_bf16, lo)


def _dot_sel_lhs(x, sel_bf16):
    hi, mid, lo = _split3(x)
    return _dot(hi, sel_bf16) + _dot(mid, sel_bf16) + _dot(lo, sel_bf16)


def _log_sigmoid(x):
    return -(jnp.maximum(-x, 0.0) + jnp.log1p(jnp.exp(-jnp.abs(x))))


def _group_rms(raw, a_ref):
    sq = raw * raw
    hi = sq.astype(BF16)
    lo = (sq - hi.astype(F32)).astype(BF16)
    return _dot(hi, a_ref[...]) + _dot(lo, a_ref[...])


P_QA, P_KA, P_FK, P_FV, P_GQ, P_GK, P_GV, P_GR, P_SM = 0, 1024, 2048, 2560, 3072, 3328, 3584, 4096, 4608
P_NC = 4736
S_FQ, S_FK, S_FV, S_GQ, S_GK, S_GV, S_GR, S_SM = 0, 512, 1024, 1536, 1792, 2048, 2560, 3072
S_NC = 3200


def _proj_common(xn, w_ref, o_fk, o_fv, o_gq, o_gk, o_gv, o_gr, sm, a_ref, knc_ref, wg_ref, bg_ref,
                 fk_ref, fv_ref, gq_ref, gk_ref, gvb_ref, la_ref, gr_ref):
    fk_raw = _dot(xn, w_ref[:, o_fk:o_fk + 512])
    fk_ref[0] = (fk_raw * lax.rsqrt(_group_rms(fk_raw, a_ref) + RMS_EPS)) * knc_ref[...]
    fv = _dot(xn, w_ref[:, o_fv:o_fv + 512])
    fv_ref[0] = fv
    gq_ref[0] = _dot(xn, w_ref[:, o_gq:o_gq + 256]) * GLA_SCALE
    gk_ref[0] = _dot(xn, w_ref[:, o_gk:o_gk + 256])
    gvb_ref[0] = _dot(xn, w_ref[:, o_gv:o_gv + 512]).astype(BF16)
    gr_ref[0] = _dot(xn, w_ref[:, o_gr:o_gr + 512])
    la_ref[0] = _log_sigmoid(_dot(sm.astype(BF16), wg_ref[...]) + bg_ref[...]) * (1.0 / GLA_GATE_TEMP)
    return fv


def _xnorm(x_ref, nm_ref):
    x = x_ref[0]
    ms = jnp.mean(x * x, axis=-1, keepdims=True)
    return ((x * lax.rsqrt(ms + RMS_EPS)) * nm_ref[...]).astype(BF16)


def _proj_prompt_kernel(x_ref, nm_ref, w_ref, bf_ref, qn_ref, kn_ref, knc_ref, a_ref, wg_ref, bg_ref, tri_ref,
                        qaug_ref, kaug_ref, fk_ref, fv_ref, fvb_ref, lf_ref, gq_ref, gk_ref, gvb_ref, la_ref,
                        gr_ref, carry):
    tm = x_ref.shape[1]

    @pl.when(pl.program_id(1) == 0)
    def _():
        carry[...] = jnp.zeros_like(carry)

    xn = _xnorm(x_ref, nm_ref)
    sm = _dot(xn, w_ref[:, P_SM:P_SM + LANES])
    lf = _log_sigmoid(sm + bf_ref[...])
    lf_ref[0] = lf
    cum = _dot_sel_rhs(tri_ref[...], lf) + carry[...]
    carry[...] = cum[tm - 1:tm, :]

    fv = _proj_common(xn, w_ref, P_FK, P_FV, P_GQ, P_GK, P_GV, P_GR, sm, a_ref, knc_ref, wg_ref, bg_ref,
                      fk_ref, fv_ref, gq_ref, gk_ref, gvb_ref, la_ref, gr_ref)
    fvb_ref[0] = fv.astype(BF16)

    qa_all = _dot(xn, w_ref[:, P_QA:P_QA + 1024])
    ka_all = _dot(xn, w_ref[:, P_KA:P_KA + 1024])
    lane = lax.broadcasted_iota(I32, (tm, LANES), 1)
    inv = 1.0 / FOX_HEAD_DIM
    for h in range(N_FOX_HEADS):
        sl = slice(h * LANES, (h + 1) * LANES)
        qa = qa_all[:, sl]
        qn = (qa * lax.rsqrt(jnp.sum(qa * qa, axis=-1, keepdims=True) * inv + RMS_EPS)) * qn_ref[...]
        qn = jnp.where(lane >= 64, jnp.where(lane < 67, -1.0, 0.0), qn)
        qaug_ref[0, :, sl] = qn.astype(BF16)
        ka = ka_all[:, sl]
        kn = (ka * lax.rsqrt(jnp.sum(ka * ka, axis=-1, keepdims=True) * inv + RMS_EPS)) * kn_ref[...]
        c = cum[:, h:h + 1]
        c_hi = c.astype(BF16).astype(F32)
        r = c - c_hi
        c_mid = r.astype(BF16).astype(F32)
        c_lo = r - c_mid
        kn = jnp.where(lane == 64, c_hi, jnp.where(lane == 65, c_mid, jnp.where(lane == 66, c_lo, kn)))
        kaug_ref[0, :, sl] = kn.astype(BF16)


def _proj_sample_kernel(x_ref, nm_ref, w_ref, bf_ref, qnc_ref, knc_ref, a_ref, wg_ref, bg_ref,
                        fq_ref, fk_ref, fv_ref, lf_ref, gq_ref, gk_ref, gvb_ref, la_ref, gr_ref):
    xn = _xnorm(x_ref, nm_ref)
    sm = _dot(xn, w_ref[:, S_SM:S_SM + LANES])
    lf_ref[0] = _log_sigmoid(sm + bf_ref[...])
    fq_raw = _dot(xn, w_ref[:, S_FQ:S_FQ + 512])
    fq_ref[0] = (fq_raw * lax.rsqrt(_group_rms(fq_raw, a_ref) + RMS_EPS)) * qnc_ref[...]
    _proj_common(xn, w_ref, S_FK, S_FV, S_GQ, S_GK, S_GV, S_GR, sm, a_ref, knc_ref, wg_ref, bg_ref,
                 fk_ref, fv_ref, gq_ref, gk_ref, gvb_ref, la_ref, gr_ref)


def _full(shape):
    nd = len(shape)
    return pl.BlockSpec(shape, lambda *_: (0,) * nd)


def _proj_weights(w_in, b_forget, q_norm, k_norm, w_gate_up, b_gate, prompt):
    o = IN_OFFS
    wq, wk, wv, wf = (w_in[:, o[i]:o[i + 1]] for i in range(4))
    wgq, wgk, wgv, wga, wgr = (w_in[:, o[i]:o[i + 1]] for i in range(4, 9))
    small = jnp.concatenate([wf, wga, jnp.zeros((D_MODEL, LANES - 24), F32)], axis=1)

    def expand(w):
        w3 = w.reshape(D_MODEL, N_FOX_HEADS, FOX_HEAD_DIM)
        return jnp.concatenate([w3, jnp.zeros_like(w3)], axis=-1).reshape(D_MODEL, N_FOX_HEADS * LANES)

    if prompt:
        cols = [expand(wq), expand(wk), wk, wv, wgq, wgk, wgv, wgr, small]
    else:
        cols = [wq, wk, wv, wgq, wgk, wgv, wgr, small]
    w = jnp.concatenate(cols, axis=1).astype(BF16)
    bf = jnp.concatenate([b_forget, jnp.zeros((LANES - N_FOX_HEADS,), F32)])[None]
    zpad = jnp.zeros((LANES - FOX_HEAD_DIM,), F32)
    qn_pad = jnp.concatenate([q_norm * FOX_SCALE, zpad])[None]
    kn_pad = jnp.concatenate([k_norm, zpad])[None]
    qnc = jnp.tile(q_norm * FOX_SCALE, N_FOX_HEADS)[None]
    knc = jnp.tile(k_norm, N_FOX_HEADS)[None]
    g = np.arange(FOX_WIDTH) // FOX_HEAD_DIM
    a = jnp.asarray((g[:, None] == g[None, :]).astype(np.float32) / FOX_HEAD_DIM, BF16)
    wg = jnp.zeros((LANES, GLA_QK_WIDTH), F32).at[N_FOX_HEADS:N_FOX_HEADS + GLA_GATE_RANK].set(w_gate_up).astype(BF16)
    return w, bf, qn_pad, kn_pad, qnc, knc, a, wg, b_gate[None]


def _proj_prompt(x, norm_mix, wts, tm):
    b, s, _ = x.shape
    w, bf, qn_pad, kn_pad, _, knc, a, wg, bg = wts
    tri = jnp.asarray(np.tril(np.ones((tm, tm), np.float32)), BF16)
    tok = lambda width: pl.BlockSpec((1, tm, width), lambda i, j: (i, j, 0))
    shp = lambda width, dt: jax.ShapeDtypeStruct((b, s, width), dt)
    return pl.pallas_call(
        _proj_prompt_kernel,
        grid=(b, s // tm),
        in_specs=[tok(D_MODEL), _full((1, D_MODEL)), _full(w.shape), _full((1, LANES)), _full((1, LANES)),
                  _full((1, LANES)), _full((1, FOX_WIDTH)), _full(a.shape), _full(wg.shape),
                  _full((1, GLA_QK_WIDTH)), _full((tm, tm))],
        out_specs=[tok(1024), tok(1024), tok(512), tok(512), tok(512), tok(LANES), tok(256), tok(256), tok(512),
                   tok(256), tok(512)],
        out_shape=[shp(1024, BF16), shp(1024, BF16), shp(512, F32), shp(512, F32), shp(512, BF16), shp(LANES, F32),
                   shp(256, F32), shp(256, F32), shp(512, BF16), shp(256, F32), shp(512, F32)],
        scratch_shapes=[pltpu.VMEM((1, LANES), F32)],
        compiler_params=_cparams(("arbitrary", "arbitrary")),
        name="proj_prompt",
    )(x, norm_mix[None], w, bf, qn_pad, kn_pad, knc, a, wg, bg, tri)


def _proj_sample(x, norm_mix, wts, tm):
    b, s, _ = x.shape
    w, bf, _, _, qnc, knc, a, wg, bg = wts
    tok = lambda width: pl.BlockSpec((1, tm, width), lambda i, j: (i, j, 0))
    shp = lambda width, dt: jax.ShapeDtypeStruct((b, s, width), dt)
    return pl.pallas_call(
        _proj_sample_kernel,
        grid=(b, s // tm),
        in_specs=[tok(D_MODEL), _full((1, D_MODEL)), _full(w.shape), _full((1, LANES)), _full((1, FOX_WIDTH)),
                  _full((1, FOX_WIDTH)), _full(a.shape), _full(wg.shape), _full((1, GLA_QK_WIDTH))],
        out_specs=[tok(512), tok(512), tok(512), tok(LANES), tok(256), tok(256), tok(512), tok(256), tok(512)],
        out_shape=[shp(512, F32), shp(512, F32), shp(512, F32), shp(LANES, F32), shp(256, F32), shp(256, F32),
                   shp(512, BF16), shp(256, F32), shp(512, F32)],
        compiler_params=_cparams(("arbitrary", "arbitrary")),
        name="proj_sample",
    )(x, norm_mix[None], w, bf, qnc, knc, a, wg, bg)


def _fox_prompt_kernel(kaug_ref, qaug_ref, vt_ref, o_ref):
    tq = qaug_ref.shape[1]
    tk = vt_ref.shape[4]
    qi = pl.program_id(2)
    q = qaug_ref[0]

    def update(st, vt, carry):
        m, l, acc = carry
        m_new = jnp.maximum(m, jnp.max(st, axis=0, keepdims=True))
        alpha = jnp.exp(m - m_new)
        p = jnp.exp(st - m_new)
        l = alpha * l + jnp.sum(p, axis=0, keepdims=True)
        acc = alpha * acc + _dot(vt, p.astype(BF16))
        return m_new, l, acc

    def body(kj, carry):
        k = kaug_ref[0, pl.ds(pl.multiple_of(kj * tk, tk), tk), :]
        return update(_dot_nt(k, q), vt_ref[0, 0, kj], carry)

    init = (jnp.full((1, tq), NEG_INF, F32), jnp.zeros((1, tq), F32), jnp.zeros((FOX_HEAD_DIM, tq), F32))
    carry = lax.fori_loop(0, qi, body, init)
    k = kaug_ref[0, pl.ds(pl.multiple_of(qi * tk, tk), tk), :]
    st = _dot_nt(k, q)
    row = lax.broadcasted_iota(I32, (tk, tq), 0)
    col = lax.broadcasted_iota(I32, (tk, tq), 1)
    st = jnp.where(row <= col, st, NEG_INF)
    _, l, acc = update(st, vt_ref[0, 0, qi], carry)
    o_ref[0, 0] = (acc / l).astype(BF16)


def _fox_prompt(qaug, kaug, fvb, t):
    b, s, _ = qaug.shape
    n = s // t
    vt = fvb.reshape(b, n, t, N_FOX_HEADS, FOX_HEAD_DIM).transpose(0, 3, 1, 4, 2)
    ot = pl.pallas_call(
        _fox_prompt_kernel,
        grid=(b, N_FOX_HEADS, n),
        in_specs=[pl.BlockSpec((1, s, LANES), lambda i, h, q: (i, 0, h)),
                  pl.BlockSpec((1, t, LANES), lambda i, h, q: (i, q, h)),
                  pl.BlockSpec((1, 1, n, FOX_HEAD_DIM, t), lambda i, h, q: (i, h, 0, 0, 0))],
        out_specs=pl.BlockSpec((1, 1, FOX_HEAD_DIM, t), lambda i, h, q: (i, h, 0, q)),
        out_shape=jax.ShapeDtypeStruct((b, N_FOX_HEADS, FOX_HEAD_DIM, s), BF16),
        compiler_params=_cparams(("arbitrary", "arbitrary", "arbitrary")),
        name="fox_prompt",
    )(kaug, qaug, vt)
    return ot.transpose(0, 3, 1, 2).reshape(b, s, FOX_WIDTH)


def _gla_gate_out(o, gr, gn_ref):
    on = (o * lax.rsqrt(jnp.mean(o * o, axis=-1, keepdims=True) + RMS_EPS)) * gn_ref[...]
    return on * (gr * jax.nn.sigmoid(gr))


def _gla_prompt_kernel(gq_ref, gk_ref, la_ref, gkt_ref, lat_ref, gv_ref, gr_ref, gn_ref, tril_ref, triu_ref,
                       g_ref, sn_ref, state):
    c = GLA_CHUNK
    nch = gq_ref.shape[1] // c

    @pl.when(pl.program_id(1) == 0)
    def _():
        state[...] = jnp.zeros_like(state)

    lane = lax.broadcasted_iota(I32, (c, GLA_QK_WIDTH), 1)
    row = lax.broadcasted_iota(I32, (c, c), 0)
    col = lax.broadcasted_iota(I32, (c, c), 1)
    causal = col <= row
    for ci in range(nch):
        rs = slice(ci * c, (ci + 1) * c)
        q = gq_ref[0, rs, :]
        k = gk_ref[0, rs, :]
        v = gv_ref[0, rs, :]
        cum = _dot_sel_rhs(tril_ref[...], la_ref[0, rs, :])
        cum_t = _dot_sel_lhs(lat_ref[0, ci], triu_ref[...])
        last_col = cum_t[:, c - 1:c]
        mid = cum[c // 2 - 1:c // 2, :]
        qs = (q * jnp.exp(cum - mid))
        ks = (k * jnp.exp(mid - cum)).astype(BF16)
        qd = q * jnp.exp(cum)
        kh_t = (gkt_ref[0, ci] * jnp.exp(last_col - cum_t)).astype(BF16)
        st = state[...]
        st_b = st.astype(BF16)
        upd = _dot(kh_t, v)
        new_rows = []
        for h in range(N_GLA_HEADS):
            hm = (lane >= h * GLA_DK) & (lane < (h + 1) * GLA_DK)
            sc = _dot_nt(jnp.where(hm, qs, 0.0).astype(BF16), ks)
            sc = jnp.where(causal, sc, 0.0).astype(BF16)
            vs = slice(h * GLA_DV, (h + 1) * GLA_DV)
            o = _dot(sc, v[:, vs]) + _dot(jnp.where(hm, qd, 0.0).astype(BF16), st_b)
            g_ref[0, rs, vs] = _gla_gate_out(o, gr_ref[0, rs, vs], gn_ref).astype(BF16)
            new_rows.append(upd[h * GLA_DK:(h + 1) * GLA_DK, vs])
        state[...] = jnp.exp(last_col) * st + jnp.concatenate(new_rows, axis=0)

    @pl.when(pl.program_id(1) == pl.num_programs(1) - 1)
    def _():
        sn_ref[0] = state[...]


def _gla_prompt(gq, gk, la, gvb, gr, gla_norm, tg):
    b, s, _ = gq.shape
    c = GLA_CHUNK
    nch = tg // c
    to_t = lambda a: a.reshape(b, s // c, c, GLA_QK_WIDTH).transpose(0, 1, 3, 2)
    tril = jnp.asarray(np.tril(np.ones((c, c), np.float32)), BF16)
    triu = jnp.asarray(np.triu(np.ones((c, c), np.float32)), BF16)
    tok = lambda width: pl.BlockSpec((1, tg, width), lambda i, j: (i, j, 0))
    tr = pl.BlockSpec((1, nch, GLA_QK_WIDTH, c), lambda i, j: (i, j, 0, 0))
    g, sn = pl.pallas_call(
        _gla_prompt_kernel,
        grid=(b, s // tg),
        in_specs=[tok(256), tok(256), tok(256), tr, tr, tok(512), tok(512), _full((1, GLA_DV)), _full((c, c)),
                  _full((c, c))],
        out_specs=[tok(512), pl.BlockSpec((1, GLA_QK_WIDTH, GLA_DV), lambda i, j: (i, 0, 0))],
        out_shape=[jax.ShapeDtypeStruct((b, s, GLA_V_WIDTH), BF16),
                   jax.ShapeDtypeStruct((b, GLA_QK_WIDTH, GLA_DV), F32)],
        scratch_shapes=[pltpu.VMEM((GLA_QK_WIDTH, GLA_DV), F32)],
        compiler_params=_cparams(("arbitrary", "arbitrary")),
        name="gla_prompt",
    )(gq, gk, la, to_t(gk), to_t(la), gvb, gr, gla_norm[None], tril, triu)
    return g, sn.reshape(b, N_GLA_HEADS, GLA_DK, GLA_DV)


def _gla_sample_kernel(qt_ref, kt_ref, lt_ref, v_ref, gr_ref, gn_ref, s0_ref, g_ref, sn_ref):
    ds = v_ref.shape[1]
    for h in range(N_GLA_HEADS):
        st = s0_ref[0, h]
        rs = slice(h * GLA_DK, (h + 1) * GLA_DK)
        vs = slice(h * GLA_DV, (h + 1) * GLA_DV)
        for t in range(ds):
            a = jnp.exp(lt_ref[0, rs, t:t + 1])
            st = a * st + kt_ref[0, rs, t:t + 1] * v_ref[0, t:t + 1, vs].astype(F32)
            o = jnp.sum(qt_ref[0, rs, t:t + 1] * st, axis=0, keepdims=True)
            g_ref[0, t:t + 1, vs] = _gla_gate_out(o, gr_ref[0, t:t + 1, vs], gn_ref)
        sn_ref[0, h] = st


def _gla_sample(gq, gk, la, gv, gr, gla_norm, s0):
    b, ds, _ = gq.shape
    t3 = lambda a: a.transpose(0, 2, 1)
    colb = pl.BlockSpec((1, GLA_QK_WIDTH, ds), lambda i: (i, 0, 0))
    rowb = pl.BlockSpec((1, ds, GLA_V_WIDTH), lambda i: (i, 0, 0))
    stb = pl.BlockSpec((1, N_GLA_HEADS, GLA_DK, GLA_DV), lambda i: (i, 0, 0, 0))
    return pl.pallas_call(
        _gla_sample_kernel,
        grid=(b,),
        in_specs=[colb, colb, colb, rowb, rowb, _full((1, GLA_DV)), stb],
        out_specs=[rowb, stb],
        out_shape=[jax.ShapeDtypeStruct((b, ds, GLA_V_WIDTH), F32),
                   jax.ShapeDtypeStruct((b, N_GLA_HEADS, GLA_DK, GLA_DV), F32)],
        compiler_params=_cparams(("arbitrary",)),
        name="gla_sample",
    )(t3(gq), t3(gk), t3(la), gv, gr, gla_norm[None], s0)


def _fox_sample_kernel(pt_ref, *refs, pp, ds):
    del pt_ref
    ck = refs[0:pp]
    cv = refs[pp:2 * pp]
    cl = refs[2 * pp:3 * pp]
    q_ref, kn_ref, vn_ref, lfn_ref, mgt_ref, bm_ref, o_ref, m_s, l_s, acc_s, carry_s, cn_s = refs[3 * pp:]
    nrow = ds * N_FOX_HEADS
    p_id = pl.program_id(1)

    @pl.when(p_id == 0)
    def _():
        m_s[...] = jnp.full_like(m_s, NEG_INF)
        l_s[...] = jnp.zeros_like(l_s)
        acc_s[...] = jnp.zeros_like(acc_s)
        carry_s[...] = jnp.zeros_like(carry_s)
        run = lfn_ref[0, 0:N_FOX_HEADS, :]
        cn_s[0:N_FOX_HEADS, :] = run
        for t in range(1, ds):
            run = run + lfn_ref[0, t * N_FOX_HEADS:(t + 1) * N_FOX_HEADS, :]
            cn_s[t * N_FOX_HEADS:(t + 1) * N_FOX_HEADS, :] = run

    qf = q_ref[0]
    qb = qf.astype(BF16)
    cn = cn_s[...]

    def flash(logits, pv_fn):
        m = m_s[...]
        m_new = jnp.maximum(m, jnp.max(logits, axis=1, keepdims=True))
        alpha = jnp.exp(m - m_new)
        p = jnp.exp(logits - m_new)
        l_s[...] = alpha * l_s[...] + jnp.sum(p, axis=1, keepdims=True)
        acc_s[...] = alpha * acc_s[...] + pv_fn(p)
        m_s[...] = m_new

    for i in range(pp):
        lt = cl[i][0]
        excl = _dot_sel_lhs(lt, mgt_ref[...]) + carry_s[...]
        carry_s[...] = carry_s[...] + jnp.sum(lt, axis=1, keepdims=True)
        s = _dot_nt(qb, ck[i][0].astype(BF16))
        logits = s + cn + jnp.concatenate([excl] * ds, axis=0)
        vb = cv[i][0].astype(BF16)
        flash(logits, lambda p: _dot(p.astype(BF16), vb))

    @pl.when(p_id == pl.num_programs(1) - 1)
    def _():
        rowi = lax.broadcasted_iota(I32, (nrow, 1), 0)
        for sp in range(ds):
            kn = kn_ref[0, sp:sp + 1, :]
            vn = vn_ref[0, sp:sp + 1, :]
            s = jnp.sum(qf * kn, axis=1, keepdims=True)
            cn_key = jnp.concatenate([cn_s[sp * N_FOX_HEADS:(sp + 1) * N_FOX_HEADS, :]] * ds, axis=0)
            logits = jnp.where(rowi >= sp * N_FOX_HEADS, s + cn - cn_key, NEG_INF)
            flash(logits, lambda p: p * vn)
        res = (acc_s[...] / l_s[...]) * bm_ref[...]
        for t in range(ds):
            o_ref[0, t:t + 1, :] = jnp.sum(res[t * N_FOX_HEADS:(t + 1) * N_FOX_HEADS, :], axis=0, keepdims=True)


def _fox_sample(fq, fk, fv, lf, cache_k, cache_v, cache_logf, page_table, pp):
    db, ds, _ = fq.shape
    n_pool = cache_k.shape[0]
    n_pages = page_table.shape[1]
    nrow = ds * N_FOX_HEADS
    ck = cache_k.reshape(n_pool, PAGE_SIZE, FOX_WIDTH)
    cv = cache_v.reshape(n_pool, PAGE_SIZE, FOX_WIDTH)
    clt = cache_logf.reshape(n_pool, PAGE_SIZE, N_FOX_HEADS).transpose(0, 2, 1)
    hmask = (np.arange(FOX_WIDTH)[None, :] // FOX_HEAD_DIM == np.arange(N_FOX_HEADS)[:, None]).astype(np.float32)
    bm = jnp.asarray(np.tile(hmask, (ds, 1)))
    qbd = (fq[:, :, None, :] * jnp.asarray(hmask)[None, None]).reshape(db, nrow, FOX_WIDTH)
    lfn = lf.reshape(db, nrow, 1)
    mgt = jnp.asarray(np.tril(np.ones((PAGE_SIZE, PAGE_SIZE), np.float32), -1), BF16)

    def page_spec(i, shape):
        return pl.BlockSpec(shape, lambda b, p, pt: (pt[b, n_pages - 1 - (p * pp + i)], 0, 0))

    per_b = lambda shape: pl.BlockSpec(shape, lambda b, p, pt: (b, 0, 0))
    in_specs = ([page_spec(i, (1, PAGE_SIZE, FOX_WIDTH)) for i in range(pp)]
                + [page_spec(i, (1, PAGE_SIZE, FOX_WIDTH)) for i in range(pp)]
                + [page_spec(i, (1, N_FOX_HEADS, PAGE_SIZE)) for i in range(pp)]
                + [per_b((1, nrow, FOX_WIDTH)), per_b((1, ds, FOX_WIDTH)), per_b((1, ds, FOX_WIDTH)),
                   per_b((1, nrow, 1)),
                   pl.BlockSpec((PAGE_SIZE, PAGE_SIZE), lambda b, p, pt: (0, 0)),
                   pl.BlockSpec((nrow, FOX_WIDTH), lambda b, p, pt: (0, 0))])
    return pl.pallas_call(
        functools.partial(_fox_sample_kernel, pp=pp, ds=ds),
        grid_spec=pltpu.PrefetchScalarGridSpec(
            num_scalar_prefetch=1,
            grid=(db, n_pages // pp),
            in_specs=in_specs,
            out_specs=per_b((1, ds, FOX_WIDTH)),
            scratch_shapes=[pltpu.VMEM((nrow, 1), F32), pltpu.VMEM((nrow, 1), F32),
                            pltpu.VMEM((nrow, FOX_WIDTH), F32), pltpu.VMEM((N_FOX_HEADS, 1), F32),
                            pltpu.VMEM((nrow, 1), F32)]),
        out_shape=jax.ShapeDtypeStruct((db, ds, FOX_WIDTH), F32),
        compiler_params=_cparams(("arbitrary", "arbitrary")),
        name="fox_sample",
    )(page_table, *([ck] * pp), *([cv] * pp), *([clt] * pp), qbd, fk, fv, lfn, mgt, bm)


def _topk_rows(s, k):
    r = s.shape[0]
    iota = lax.broadcasted_iota(I32, s.shape, 0)
    vals, idxs = [], []
    for _ in range(k):
        m = jnp.max(s, axis=0, keepdims=True)
        i = jnp.min(jnp.where(s == m, iota, r), axis=0, keepdims=True)
        s = jnp.where(iota == i, -jnp.inf, s)
        vals.append(m)
        idxs.append(i)
    return jnp.concatenate(vals, axis=0), jnp.concatenate(idxs, axis=0)


def _route_kernel(x_ref, o_ref, g_ref, wo_ref, nf_ref, wq_ref, kbt_ref, h_ref, xn_ref, idx_ref, gate_ref):
    kk = PEER_TOPK
    h = x_ref[...] + _dot(o_ref[...], wo_ref[0:FOX_WIDTH, :]) + _dot(g_ref[...], wo_ref[FOX_WIDTH:, :])
    h_ref[...] = h
    xn = (h * lax.rsqrt(jnp.mean(h * h, axis=-1, keepdims=True) + RMS_EPS)) * nf_ref[...]
    xn_ref[...] = xn
    q = _dot(xn.astype(BF16), wq_ref[...]).astype(BF16)
    st = _dot_nt(kbt_ref[...], q)
    tm = st.shape[1]
    for hd in range(PEER_HEADS):
        base = hd * 2 * PEER_N_KEYS
        ts0, ti0 = _topk_rows(st[base:base + PEER_N_KEYS, :], kk)
        ts1, ti1 = _topk_rows(st[base + PEER_N_KEYS:base + 2 * PEER_N_KEYS, :], kk)
        cand = jnp.concatenate([ts0[a:a + 1, :] + ts1 for a in range(kk)], axis=0)
        score, sel = _topk_rows(cand, kk)
        sa = sel >> 4
        sb = sel & (kk - 1)
        i1 = jnp.zeros((kk, tm), I32)
        i2 = jnp.zeros((kk, tm), I32)
        for a in range(kk):
            i1 = jnp.where(sa == a, ti0[a:a + 1, :], i1)
            i2 = jnp.where(sb == a, ti1[a:a + 1, :], i2)
        e = jnp.exp(score - jnp.max(score, axis=0, keepdims=True))
        gate = e / jnp.sum(e, axis=0, keepdims=True)
        idx_ref[hd * kk:(hd + 1) * kk, :] = i1 * PEER_N_KEYS + i2
        gate_ref[hd * kk:(hd + 1) * kk, :] = gate


def _route(x, o, g, w_out, norm_ffn, peer_wq, peer_sub_keys, tm):
    t = x.shape[0]
    nb = PEER_HEADS * 2
    eye = jnp.eye(nb, dtype=F32)
    kbt = (peer_sub_keys.reshape(nb, PEER_N_KEYS, 1, PEER_HALF) * eye[:, None, :, None]).reshape(
        nb * PEER_N_KEYS, nb * PEER_HALF).astype(BF16)
    tok = lambda width: pl.BlockSpec((tm, width), lambda i: (i, 0))
    colb = pl.BlockSpec((PEER_SLOTS, tm), lambda i: (0, i))
    return pl.pallas_call(
        _route_kernel,
        grid=(t // tm,),
        in_specs=[tok(D_MODEL), tok(FOX_WIDTH), tok(GLA_V_WIDTH), _full((D_MODEL, D_MODEL)), _full((1, D_MODEL)),
                  _full((D_MODEL, D_MODEL)), _full(kbt.shape)],
        out_specs=[tok(D_MODEL), tok(D_MODEL), colb, colb],
        out_shape=[jax.ShapeDtypeStruct((t, D_MODEL), F32), jax.ShapeDtypeStruct((t, D_MODEL), F32),
                   jax.ShapeDtypeStruct((PEER_SLOTS, t), I32), jax.ShapeDtypeStruct((PEER_SLOTS, t), F32)],
        compiler_params=_cparams(("arbitrary",)),
        name="route",
    )(x, o, g, w_out.astype(BF16), norm_ffn[None], peer_wq.astype(BF16), kbt)


def _pack_table(tab):
    e = tab.shape[0]
    b = lax.bitcast_convert_type(tab.astype(BF16), jnp.uint16).astype(jnp.uint32)
    w = b[:, :D_MODEL // 2] | (b[:, D_MODEL // 2:] << 16)
    return lax.bitcast_convert_type(w, I32).reshape(e * 4, LANES)


def _unpack(tile):
    lo = lax.bitcast_convert_type(tile << 16, F32)
    hi = lax.bitcast_convert_type(tile & jnp.int32(-65536), F32)
    return lo, hi


def _gelu(x):
    return 0.5 * x * (1.0 + lax.erf(x * (2.0 ** -0.5)))


def _peer_u_kernel(off_ref, x_ref, par_ref, gate_ref, tab_ref, wc_ref):
    tb = x_ref.shape[0]
    sub = lax.broadcasted_iota(I32, (SUBLANES, LANES), 0)
    lane = lax.broadcasted_iota(I32, (SUBLANES, LANES), 1)
    low = sub < 4

    def token(t, _):
        x8 = x_ref[t]
        xr = pltpu.roll(x8, 4, 0)
        xlo = jnp.where(low, x8, xr)
        xhi = jnp.where(low, xr, x8)
        r8 = jnp.zeros((SUBLANES, LANES), F32)
        for j in range(PEER_SLOTS):
            off = pl.multiple_of(off_ref[t, j], SUBLANES)
            lo, hi = _unpack(tab_ref[pl.ds(off, SUBLANES), :])
            r = jnp.sum(lo * xlo + hi * xhi, axis=1, keepdims=True)
            r8 = jnp.where(lane == j, r, r8)
        a0 = r8[0:1] + r8[1:2] + r8[2:3] + r8[3:4]
        a1 = r8[4:5] + r8[5:6] + r8[6:7] + r8[7:8]
        odd = par_ref[pl.ds(t, 1), :] == 1
        w = gate_ref[pl.ds(t, 1), :] * _gelu(jnp.where(odd, a1, a0))
        wc_ref[t] = jnp.where(low, jnp.where(odd, 0.0, w), jnp.where(odd, w, 0.0))
        return 0

    lax.fori_loop(0, tb, token, 0)


def _peer_v_kernel(off_ref, wc_ref, h_ref, tab_ref, y_ref):
    tb = wc_ref.shape[0]
    sub = lax.broadcasted_iota(I32, (SUBLANES, LANES), 0)
    low = sub < 4
    nacc = 4

    def token(t, _):
        wc = wc_ref[t]
        acc_lo = [jnp.zeros((SUBLANES, LANES), F32) for _ in range(nacc)]
        acc_hi = [jnp.zeros((SUBLANES, LANES), F32) for _ in range(nacc)]
        for j in range(PEER_SLOTS):
            off = pl.multiple_of(off_ref[t, j], SUBLANES)
            lo, hi = _unpack(tab_ref[pl.ds(off, SUBLANES), :])
            wv = jnp.broadcast_to(wc[:, j:j + 1], (SUBLANES, LANES))
            acc_lo[j % nacc] = acc_lo[j % nacc] + lo * wv
            acc_hi[j % nacc] = acc_hi[j % nacc] + hi * wv
        alo = (acc_lo[0] + acc_lo[1]) + (acc_lo[2] + acc_lo[3])
        ahi = (acc_hi[0] + acc_hi[1]) + (acc_hi[2] + acc_hi[3])
        ylo = alo + pltpu.roll(alo, 4, 0)
        yhi = ahi + pltpu.roll(ahi, 4, 0)
        y_ref[t] = h_ref[t] + jnp.where(low, ylo, yhi)
        return 0

    lax.fori_loop(0, tb, token, 0)


def _peer_gather(h, xn, idx_t, gate_t, tab_u, tab_v, tb):
    t = h.shape[0]
    idx = idx_t.T
    gate = gate_t.T
    off = (idx >> 1) << 3
    par = idx & 1
    x8 = xn.reshape(t, SUBLANES, LANES)
    h8 = h.reshape(t, SUBLANES, LANES)
    smem = pl.BlockSpec((tb, PEER_SLOTS), lambda i: (i, 0), memory_space=pltpu.SMEM)
    tok3 = pl.BlockSpec((tb, SUBLANES, LANES), lambda i: (i, 0, 0))
    tok2 = pl.BlockSpec((tb, PEER_SLOTS), lambda i: (i, 0))
    tabs = pl.BlockSpec(tab_u.shape, lambda i: (0, 0), pipeline_mode=pl.Buffered(1))
    wc = pl.pallas_call(
        _peer_u_kernel,
        grid=(t // tb,),
        in_specs=[smem, tok3, tok2, tok2, tabs],
        out_specs=tok3,
        out_shape=jax.ShapeDtypeStruct((t, SUBLANES, LANES), F32),
        compiler_params=_cparams(("arbitrary",)),
        name="peer_u",
    )(off, x8, par, gate, tab_u)
    y8 = pl.pallas_call(
        _peer_v_kernel,
        grid=(t // tb,),
        in_specs=[smem, tok3, tok3, tabs],
        out_specs=tok3,
        out_shape=jax.ShapeDtypeStruct((t, SUBLANES, LANES), F32),
        compiler_params=_cparams(("arbitrary",)),
        name="peer_v",
    )(off, wc, h8, tab_v)
    return y8.reshape(t, D_MODEL)


def _pick(n, pref):
    t = min(n, pref)
    while n % t:
        t //= 2
    return t


def kernel(x_prompt, x_sample, cache_k, cache_v, cache_logf, state_gla, page_table, norm_mix, w_in, b_forget,
           q_norm, k_norm, w_gate_up, b_gate, gla_norm, w_out, norm_ffn, peer_wq, peer_sub_keys, peer_u, peer_v):
    depth = w_in.shape[0]
    assert depth == 1, "single-layer step"
    l = 0
    b, s, _ = x_prompt.shape
    db, ds, _ = x_sample.shape
    tab_u = _pack_table(peer_u[l])
    tab_v = _pack_table(peer_v[l])

    wts_p = _proj_weights(w_in[l], b_forget[l], q_norm[l], k_norm[l], w_gate_up[l], b_gate[l], prompt=True)
    (qaug, kaug, fk_p, fv_p, fvb, lf_p, gq, gk, gvb, la, gr) = _proj_prompt(x_prompt, norm_mix[l], wts_p, _pick(s, 256))
    fo_p = _fox_prompt(qaug, kaug, fvb, _pick(s, 512))
    g_p, sn_p = _gla_prompt(gq, gk, la, gvb, gr, gla_norm[l], _pick(s, 256))
    tp = b * s
    h_p, xn_p, idx_p, gate_p = _route(x_prompt.reshape(tp, D_MODEL), fo_p.reshape(tp, FOX_WIDTH),
                                      g_p.reshape(tp, GLA_V_WIDTH), w_out[l], norm_ffn[l], peer_wq[l],
                                      peer_sub_keys[l], _pick(tp, 256))
    y_p = _peer_gather(h_p, xn_p, idx_p, gate_p, tab_u, tab_v, _pick(tp, 64)).reshape(b, s, D_MODEL)

    ts = db * ds
    wts_s = _proj_weights(w_in[l], b_forget[l], q_norm[l], k_norm[l], w_gate_up[l], b_gate[l], prompt=False)
    (fq_s, fk_s, fv_s, lf_s, gq_s, gk_s, gvb_s, la_s, gr_s) = _proj_sample(
        x_sample.reshape(1, ts, D_MODEL), norm_mix[l], wts_s, _pick(ts, 256))
    r3 = lambda a: a.reshape(db, ds, a.shape[-1])
    lf_s8 = r3(lf_s)[:, :, :N_FOX_HEADS]
    n_pages = page_table.shape[1]
    fo_s = _fox_sample(r3(fq_s), r3(fk_s), r3(fv_s), lf_s8, cache_k[l], cache_v[l], cache_logf[l], page_table,
                       _pick(n_pages, 4))
    g_s, sn_s = _gla_sample(r3(gq_s), r3(gk_s), r3(la_s), r3(gvb_s), r3(gr_s), gla_norm[l], state_gla[l])
    h_s, xn_s, idx_s, gate_s = _route(x_sample.reshape(ts, D_MODEL), fo_s.reshape(ts, FOX_WIDTH).astype(BF16),
                                      g_s.reshape(ts, GLA_V_WIDTH).astype(BF16), w_out[l], norm_ffn[l],
                                      peer_wq[l], peer_sub_keys[l], _pick(ts, 256))
    y_s = _peer_gather(h_s, xn_s, idx_s, gate_s, tab_u, tab_v, _pick(ts, 64)).reshape(db, ds, D_MODEL)

    hd = (N_FOX_HEADS, FOX_HEAD_DIM)
    return (y_p, y_s,
            fk_p.reshape(1, b, s, *hd), fv_p.reshape(1, b, s, *hd), lf_p[None, :, :, :N_FOX_HEADS], sn_p[None],
            r3(fk_s).reshape(1, db, ds, *hd), r3(fv_s).reshape(1, db, ds, *hd), lf_s8[None], sn_s[None])
```

```python
import functools

import numpy as np
import jax
import jax.numpy as jnp
from jax import lax
from jax.experimental import pallas as pl
from jax.experimental.pallas import tpu as pltpu

F32 = jnp.float32
BF16 = jnp.bfloat16
I32 = jnp.int32

D_MODEL = 1024
N_FOX_HEADS = 8
FOX_HEAD_DIM = 64
N_GLA_HEADS = 4
GLA_DK = 64
GLA_DV = 128
GLA_GATE_RANK = 16
GLA_GATE_TEMP = 16.0
GLA_CHUNK = 64
PEER_HEADS = 8
PEER_N_KEYS = 128
PEER_TOPK = 16
PEER_HALF = 64
PAGE_SIZE = 128
RMS_EPS = 1e-6
NEG_INF = -1e30

FOX_WIDTH = N_FOX_HEADS * FOX_HEAD_DIM
GLA_QK_WIDTH = N_GLA_HEADS * GLA_DK
GLA_V_WIDTH = N_GLA_HEADS * GLA_DV
FOX_SCALE = FOX_HEAD_DIM ** -0.5
GLA_SCALE = GLA_DK ** -0.5
LOG2E = 1.4426950408889634
IN_SIZES = [FOX_WIDTH, FOX_WIDTH, FOX_WIDTH, N_FOX_HEADS,
            GLA_QK_WIDTH, GLA_QK_WIDTH, GLA_V_WIDTH, GLA_GATE_RANK, GLA_V_WIDTH]
IN_OFFS = [int(c) for c in np.cumsum([0] + IN_SIZES)]
PEER_SLOTS = PEER_HEADS * PEER_TOPK

LANES = 128
SUBLANES = 8
VMEM_LIMIT = 56 * 1024 * 1024


def _cparams(sem, vmem=VMEM_LIMIT):
    return pltpu.CompilerParams(dimension_semantics=sem, vmem_limit_bytes=vmem)


def _dot(a, b):
    return jnp.dot(a, b, preferred_element_type=F32)


def _dot_nt(a, b):
    return lax.dot_general(a, b, (((1,), (1,)), ((), ())), preferred_element_type=F32)


def _split3(x):
    hi = x.astype(BF16)
    r = x - hi.astype(F32)
    mid = r.astype(BF16)
    lo = (r - mid.astype(F32)).astype(BF16)
    return hi, mid, lo


def _dot_sel_rhs(sel_bf16, x):
    hi, mid, lo = _split3(x)
    return _dot(sel_bf16, hi) + _dot(sel_bf16, mid) + _dot(sel_bf16, lo)


def _dot_sel_lhs(x, sel_bf16):
    hi, mid, lo = _split3(x)
    return _dot(hi, sel_bf16) + _dot(mid, sel_bf16) + _dot(lo, sel_bf16)


def _log_sigmoid(x):
    return -(jnp.maximum(-x, 0.0) + jnp.log1p(jnp.exp(-jnp.abs(x))))


def _group_rms(raw, a_ref):
    sq = raw * raw
    hi = sq.astype(BF16)
    lo = (sq - hi.astype(F32)).astype(BF16)
    return _dot(hi, a_ref[...]) + _dot(lo, a_ref[...])


P_QA, P_KA, P_FK, P_FV, P_GQ, P_GK, P_GV, P_GR, P_SM = 0, 1024, 2048, 2560, 3072, 3328, 3584, 4096, 4608
P_NC = 4736
S_FQ, S_FK, S_FV, S_GQ, S_GK, S_GV, S_GR, S_SM = 0, 512, 1024, 1536, 1792, 2048, 2560, 3072
S_NC = 3200


def _proj_common(xn, w_ref, o_fk, o_fv, o_gq, o_gk, o_gv, o_gr, sm, a_ref, knc_ref, wg_ref, bg_ref,
                 fk_ref, fv_ref, gq_ref, gk_ref, gvb_ref, la_ref, gr_ref):
    fk_raw = _dot(xn, w_ref[:, o_fk:o_fk + 512])
    fk_ref[0] = (fk_raw * lax.rsqrt(_group_rms(fk_raw, a_ref) + RMS_EPS)) * knc_ref[...]
    fv = _dot(xn, w_ref[:, o_fv:o_fv + 512])
    fv_ref[0] = fv
    gq_ref[0] = _dot(xn, w_ref[:, o_gq:o_gq + 256]) * GLA_SCALE
    gk_ref[0] = _dot(xn, w_ref[:, o_gk:o_gk + 256])
    gvb_ref[0] = _dot(xn, w_ref[:, o_gv:o_gv + 512]).astype(BF16)
    gr_ref[0] = _dot(xn, w_ref[:, o_gr:o_gr + 512])
    la_ref[0] = _log_sigmoid(_dot(sm.astype(BF16), wg_ref[...]) + bg_ref[...]) * (1.0 / GLA_GATE_TEMP)
    return fv


def _xnorm(x_ref, nm_ref):
    x = x_ref[0]
    ms = jnp.mean(x * x, axis=-1, keepdims=True)
    return ((x * lax.rsqrt(ms + RMS_EPS)) * nm_ref[...]).astype(BF16)


def _proj_prompt_kernel(x_ref, nm_ref, w_ref, bf_ref, qn_ref, kn_ref, knc_ref, a_ref, wg_ref, bg_ref, tri_ref,
                        qaug_ref, kaug_ref, fk_ref, fv_ref, fvb_ref, lf_ref, gq_ref, gk_ref, gvb_ref, la_ref,
                        gr_ref, carry):
    tm = x_ref.shape[1]

    @pl.when(pl.program_id(1) == 0)
    def _():
        carry[...] = jnp.zeros_like(carry)

    xn = _xnorm(x_ref, nm_ref)
    sm = _dot(xn, w_ref[:, P_SM:P_SM + LANES])
    lf = _log_sigmoid(sm + bf_ref[...])
    lf_ref[0] = lf
    cum = _dot_sel_rhs(tri_ref[...], lf) + carry[...]
    carry[...] = cum[tm - 1:tm, :]

    fv = _proj_common(xn, w_ref, P_FK, P_FV, P_GQ, P_GK, P_GV, P_GR, sm, a_ref, knc_ref, wg_ref, bg_ref,
                      fk_ref, fv_ref, gq_ref, gk_ref, gvb_ref, la_ref, gr_ref)
    fvb_ref[0] = fv.astype(BF16)

    qa_all = _dot(xn, w_ref[:, P_QA:P_QA + 1024])
    ka_all = _dot(xn, w_ref[:, P_KA:P_KA + 1024])
    lane = lax.broadcasted_iota(I32, (tm, LANES), 1)
    inv = 1.0 / FOX_HEAD_DIM
    for h in range(N_FOX_HEADS):
        sl = slice(h * LANES, (h + 1) * LANES)
        qa = qa_all[:, sl]
        qn = (qa * lax.rsqrt(jnp.sum(qa * qa, axis=-1, keepdims=True) * inv + RMS_EPS)) * qn_ref[...]
        qn = jnp.where(lane >= 64, jnp.where(lane < 67, -1.0, 0.0), qn)
        qaug_ref[0, :, sl] = qn.astype(BF16)
        ka = ka_all[:, sl]
        kn = (ka * lax.rsqrt(jnp.sum(ka * ka, axis=-1, keepdims=True) * inv + RMS_EPS)) * kn_ref[...]
        c = cum[:, h:h + 1] * LOG2E
        c_hi = c.astype(BF16).astype(F32)
        r = c - c_hi
        c_mid = r.astype(BF16).astype(F32)
        c_lo = r - c_mid
        kn = jnp.where(lane == 64, c_hi, jnp.where(lane == 65, c_mid, jnp.where(lane == 66, c_lo, kn)))
        kaug_ref[0, :, sl] = kn.astype(BF16)


def _proj_sample_kernel(x_ref, nm_ref, w_ref, bf_ref, qnc_ref, knc_ref, a_ref, wg_ref, bg_ref,
                        fq_ref, fk_ref, fv_ref, lf_ref, gq_ref, gk_ref, gvb_ref, la_ref, gr_ref):
    xn = _xnorm(x_ref, nm_ref)
    sm = _dot(xn, w_ref[:, S_SM:S_SM + LANES])
    lf_ref[0] = _log_sigmoid(sm + bf_ref[...])
    fq_raw = _dot(xn, w_ref[:, S_FQ:S_FQ + 512])
    fq_ref[0] = (fq_raw * lax.rsqrt(_group_rms(fq_raw, a_ref) + RMS_EPS)) * qnc_ref[...]
    _proj_common(xn, w_ref, S_FK, S_FV, S_GQ, S_GK, S_GV, S_GR, sm, a_ref, knc_ref, wg_ref, bg_ref,
                 fk_ref, fv_ref, gq_ref, gk_ref, gvb_ref, la_ref, gr_ref)


def _full(shape):
    nd = len(shape)
    return pl.BlockSpec(shape, lambda *_: (0,) * nd)


def _proj_weights(w_in, b_forget, q_norm, k_norm, w_gate_up, b_gate, prompt):
    o = IN_OFFS
    wq, wk, wv, wf = (w_in[:, o[i]:o[i + 1]] for i in range(4))
    wgq, wgk, wgv, wga, wgr = (w_in[:, o[i]:o[i + 1]] for i in range(4, 9))
    small = jnp.concatenate([wf, wga, jnp.zeros((D_MODEL, LANES - 24), F32)], axis=1)

    def expand(w):
        w3 = w.reshape(D_MODEL, N_FOX_HEADS, FOX_HEAD_DIM)
        return jnp.concatenate([w3, jnp.zeros_like(w3)], axis=-1).reshape(D_MODEL, N_FOX_HEADS * LANES)

    if prompt:
        cols = [expand(wq), expand(wk), wk, wv, wgq, wgk, wgv, wgr, small]
    else:
        cols = [wq, wk, wv, wgq, wgk, wgv, wgr, small]
    w = jnp.concatenate(cols, axis=1).astype(BF16)
    bf = jnp.concatenate([b_forget, jnp.zeros((LANES - N_FOX_HEADS,), F32)])[None]
    zpad = jnp.zeros((LANES - FOX_HEAD_DIM,), F32)
    qn_pad = jnp.concatenate([q_norm * (FOX_SCALE * LOG2E), zpad])[None]
    kn_pad = jnp.concatenate([k_norm, zpad])[None]
    qnc = jnp.tile(q_norm * FOX_SCALE, N_FOX_HEADS)[None]
    knc = jnp.tile(k_norm, N_FOX_HEADS)[None]
    g = np.arange(FOX_WIDTH) // FOX_HEAD_DIM
    a = jnp.asarray((g[:, None] == g[None, :]).astype(np.float32) / FOX_HEAD_DIM, BF16)
    wg = jnp.zeros((LANES, GLA_QK_WIDTH), F32).at[N_FOX_HEADS:N_FOX_HEADS + GLA_GATE_RANK].set(w_gate_up).astype(BF16)
    return w, bf, qn_pad, kn_pad, qnc, knc, a, wg, b_gate[None]


def _proj_prompt(x, norm_mix, wts, tm):
    b, s, _ = x.shape
    w, bf, qn_pad, kn_pad, _, knc, a, wg, bg = wts
    tri = jnp.asarray(np.tril(np.ones((tm, tm), np.float32)), BF16)
    tok = lambda width: pl.BlockSpec((1, tm, width), lambda i, j: (i, j, 0))
    shp = lambda width, dt: jax.ShapeDtypeStruct((b, s, width), dt)
    return pl.pallas_call(
        _proj_prompt_kernel,
        grid=(b, s // tm),
        in_specs=[tok(D_MODEL), _full((1, D_MODEL)), _full(w.shape), _full((1, LANES)), _full((1, LANES)),
                  _full((1, LANES)), _full((1, FOX_WIDTH)), _full(a.shape), _full(wg.shape),
                  _full((1, GLA_QK_WIDTH)), _full((tm, tm))],
        out_specs=[tok(1024), tok(1024), tok(512), tok(512), tok(512), tok(LANES), tok(256), tok(256), tok(512),
                   tok(256), tok(512)],
        out_shape=[shp(1024, BF16), shp(1024, BF16), shp(512, F32), shp(512, F32), shp(512, BF16), shp(LANES, F32),
                   shp(256, F32), shp(256, F32), shp(512, BF16), shp(256, F32), shp(512, F32)],
        scratch_shapes=[pltpu.VMEM((1, LANES), F32)],
        compiler_params=_cparams(("arbitrary", "arbitrary")),
        name="proj_prompt",
    )(x, norm_mix[None], w, bf, qn_pad, kn_pad, knc, a, wg, bg, tri)


def _proj_sample(x, norm_mix, wts, tm):
    b, s, _ = x.shape
    w, bf, _, _, qnc, knc, a, wg, bg = wts
    tok = lambda width: pl.BlockSpec((1, tm, width), lambda i, j: (i, j, 0))
    shp = lambda width, dt: jax.ShapeDtypeStruct((b, s, width), dt)
    return pl.pallas_call(
        _proj_sample_kernel,
        grid=(b, s // tm),
        in_specs=[tok(D_MODEL), _full((1, D_MODEL)), _full(w.shape), _full((1, LANES)), _full((1, FOX_WIDTH)),
                  _full((1, FOX_WIDTH)), _full(a.shape), _full(wg.shape), _full((1, GLA_QK_WIDTH))],
        out_specs=[tok(512), tok(512), tok(512), tok(LANES), tok(256), tok(256), tok(512), tok(256), tok(512)],
        out_shape=[shp(512, F32), shp(512, F32), shp(512, F32), shp(LANES, F32), shp(256, F32), shp(256, F32),
                   shp(512, BF16), shp(256, F32), shp(512, F32)],
        compiler_params=_cparams(("arbitrary", "arbitrary")),
        name="proj_sample",
    )(x, norm_mix[None], w, bf, qnc, knc, a, wg, bg)


FOX_HEADS_PER_STEP = 4


def _fox_prompt_kernel(kaug_ref, qaug_ref, vt_ref, o_ref):
    tq = qaug_ref.shape[1]
    tk = vt_ref.shape[4]
    nh = vt_ref.shape[1]
    qi = pl.program_id(2)
    qs = [qaug_ref[0, :, g * LANES:(g + 1) * LANES] for g in range(nh)]

    def scores(kj, g):
        k = kaug_ref[0, pl.ds(pl.multiple_of(kj * tk, tk), tk), g * LANES:(g + 1) * LANES]
        return _dot_nt(k, qs[g])

    def softmax_step(st, carry):
        m, l, acc = carry
        m_new = jnp.maximum(m, jnp.max(st, axis=0, keepdims=True))
        alpha = jnp.exp2(m - m_new)
        p = jnp.exp2(st - m_new)
        return m_new, alpha * l + jnp.sum(p, axis=0, keepdims=True), alpha * acc, p.astype(BF16)

    def block(kj, carry, mask):
        sts = [scores(kj, g) for g in range(nh)]
        if mask is not None:
            sts = [jnp.where(mask, st, NEG_INF) for st in sts]
        mid = [softmax_step(sts[g], carry[g]) for g in range(nh)]
        return tuple((m, l, acc + _dot(vt_ref[0, g, kj], p)) for g, (m, l, acc, p) in enumerate(mid))

    init = tuple((jnp.full((1, tq), NEG_INF, F32), jnp.zeros((1, tq), F32), jnp.zeros((FOX_HEAD_DIM, tq), F32))
                 for _ in range(nh))
    carry = lax.fori_loop(0, qi, lambda kj, c: block(kj, c, None), init)
    row = lax.broadcasted_iota(I32, (tk, tq), 0)
    col = lax.broadcasted_iota(I32, (tk, tq), 1)
    carry = block(qi, carry, row <= col)
    for g in range(nh):
        _, l, acc = carry[g]
        o_ref[0, g] = (acc / l).astype(BF16)


def _fox_prompt(qaug, kaug, fvb, t):
    b, s, _ = qaug.shape
    n = s // t
    nh = FOX_HEADS_PER_STEP
    vt = fvb.reshape(b, n, t, N_FOX_HEADS, FOX_HEAD_DIM).transpose(0, 3, 1, 4, 2)
    ot = pl.pallas_call(
        _fox_prompt_kernel,
        grid=(b, N_FOX_HEADS // nh, n),
        in_specs=[pl.BlockSpec((1, s, nh * LANES), lambda i, h, q: (i, 0, h), pipeline_mode=pl.Buffered(1)),
                  pl.BlockSpec((1, t, nh * LANES), lambda i, h, q: (i, q, h)),
                  pl.BlockSpec((1, nh, n, FOX_HEAD_DIM, t), lambda i, h, q: (i, h, 0, 0, 0),
                               pipeline_mode=pl.Buffered(1))],
        out_specs=pl.BlockSpec((1, nh, FOX_HEAD_DIM, t), lambda i, h, q: (i, h, 0, q)),
        out_shape=jax.ShapeDtypeStruct((b, N_FOX_HEADS, FOX_HEAD_DIM, s), BF16),
        compiler_params=_cparams(("arbitrary", "arbitrary", "arbitrary")),
        name="fox_prompt",
    )(kaug, qaug, vt)
    return ot.transpose(0, 3, 1, 2).reshape(b, s, FOX_WIDTH)


def _gla_gate_out(o, gr, gn_ref):
    on = (o * lax.rsqrt(jnp.mean(o * o, axis=-1, keepdims=True) + RMS_EPS)) * gn_ref[...]
    return on * (gr * jax.nn.sigmoid(gr))


def _gla_prompt_kernel(gq_ref, gk_ref, la_ref, gkt_ref, lat_ref, gv_ref, gr_ref, gn_ref, tril_ref, triu_ref,
                       g_ref, sn_ref, state):
    c = GLA_CHUNK
    nch = gq_ref.shape[1] // c

    @pl.when(pl.program_id(1) == 0)
    def _():
        state[...] = jnp.zeros_like(state)

    lane = lax.broadcasted_iota(I32, (c, GLA_QK_WIDTH), 1)
    row = lax.broadcasted_iota(I32, (c, c), 0)
    col = lax.broadcasted_iota(I32, (c, c), 1)
    causal = col <= row
    for ci in range(nch):
        rs = slice(ci * c, (ci + 1) * c)
        q = gq_ref[0, rs, :]
        k = gk_ref[0, rs, :]
        v = gv_ref[0, rs, :]
        cum = _dot_sel_rhs(tril_ref[...], la_ref[0, rs, :])
        cum_t = _dot_sel_lhs(lat_ref[0, ci], triu_ref[...])
        last_col = cum_t[:, c - 1:c]
        mid = cum[c // 2 - 1:c // 2, :]
        qs = (q * jnp.exp(cum - mid))
        ks = (k * jnp.exp(mid - cum)).astype(BF16)
        qd = q * jnp.exp(cum)
        kh_t = (gkt_ref[0, ci] * jnp.exp(last_col - cum_t)).astype(BF16)
        st = state[...]
        st_b = st.astype(BF16)
        upd = _dot(kh_t, v)
        new_rows = []
        for h in range(N_GLA_HEADS):
            hm = (lane >= h * GLA_DK) & (lane < (h + 1) * GLA_DK)
            sc = _dot_nt(jnp.where(hm, qs, 0.0).astype(BF16), ks)
            sc = jnp.where(causal, sc, 0.0).astype(BF16)
            vs = slice(h * GLA_DV, (h + 1) * GLA_DV)
            o = _dot(sc, v[:, vs]) + _dot(jnp.where(hm, qd, 0.0).astype(BF16), st_b)
            g_ref[0, rs, vs] = _gla_gate_out(o, gr_ref[0, rs, vs], gn_ref).astype(BF16)
            new_rows.append(upd[h * GLA_DK:(h + 1) * GLA_DK, vs])
        state[...] = jnp.exp(last_col) * st + jnp.concatenate(new_rows, axis=0)

    @pl.when(pl.program_id(1) == pl.num_programs(1) - 1)
    def _():
        sn_ref[0] = state[...]


def _gla_prompt(gq, gk, la, gvb, gr, gla_norm, tg):
    b, s, _ = gq.shape
    c = GLA_CHUNK
    nch = tg // c
    to_t = lambda a: a.reshape(b, s // c, c, GLA_QK_WIDTH).transpose(0, 1, 3, 2)
    tril = jnp.asarray(np.tril(np.ones((c, c), np.float32)), BF16)
    triu = jnp.asarray(np.triu(np.ones((c, c), np.float32)), BF16)
    tok = lambda width: pl.BlockSpec((1, tg, width), lambda i, j: (i, j, 0))
    tr = pl.BlockSpec((1, nch, GLA_QK_WIDTH, c), lambda i, j: (i, j, 0, 0))
    g, sn = pl.pallas_call(
        _gla_prompt_kernel,
        grid=(b, s // tg),
        in_specs=[tok(256), tok(256), tok(256), tr, tr, tok(512), tok(512), _full((1, GLA_DV)), _full((c, c)),
                  _full((c, c))],
        out_specs=[tok(512), pl.BlockSpec((1, GLA_QK_WIDTH, GLA_DV), lambda i, j: (i, 0, 0))],
        out_shape=[jax.ShapeDtypeStruct((b, s, GLA_V_WIDTH), BF16),
                   jax.ShapeDtypeStruct((b, GLA_QK_WIDTH, GLA_DV), F32)],
        scratch_shapes=[pltpu.VMEM((GLA_QK_WIDTH, GLA_DV), F32)],
        compiler_params=_cparams(("arbitrary", "arbitrary")),
        name="gla_prompt",
    )(gq, gk, la, to_t(gk), to_t(la), gvb, gr, gla_norm[None], tril, triu)
    return g, sn.reshape(b, N_GLA_HEADS, GLA_DK, GLA_DV)


def _gla_sample_kernel(qt_ref, kt_ref, lt_ref, v_ref, gr_ref, gn_ref, s0_ref, g_ref, sn_ref):
    ds = v_ref.shape[1]
    for h in range(N_GLA_HEADS):
        st = s0_ref[0, h]
        rs = slice(h * GLA_DK, (h + 1) * GLA_DK)
        vs = slice(h * GLA_DV, (h + 1) * GLA_DV)
        for t in range(ds):
            a = jnp.exp(lt_ref[0, rs, t:t + 1])
            st = a * st + kt_ref[0, rs, t:t + 1] * v_ref[0, t:t + 1, vs].astype(F32)
            o = jnp.sum(qt_ref[0, rs, t:t + 1] * st, axis=0, keepdims=True)
            g_ref[0, t:t + 1, vs] = _gla_gate_out(o, gr_ref[0, t:t + 1, vs], gn_ref)
        sn_ref[0, h] = st


def _gla_sample(gq, gk, la, gv, gr, gla_norm, s0):
    b, ds, _ = gq.shape
    t3 = lambda a: a.transpose(0, 2, 1)
    colb = pl.BlockSpec((1, GLA_QK_WIDTH, ds), lambda i: (i, 0, 0))
    rowb = pl.BlockSpec((1, ds, GLA_V_WIDTH), lambda i: (i, 0, 0))
    stb = pl.BlockSpec((1, N_GLA_HEADS, GLA_DK, GLA_DV), lambda i: (i, 0, 0, 0))
    return pl.pallas_call(
        _gla_sample_kernel,
        grid=(b,),
        in_specs=[colb, colb, colb, rowb, rowb, _full((1, GLA_DV)), stb],
        out_specs=[rowb, stb],
        out_shape=[jax.ShapeDtypeStruct((b, ds, GLA_V_WIDTH), F32),
                   jax.ShapeDtypeStruct((b, N_GLA_HEADS, GLA_DK, GLA_DV), F32)],
        compiler_params=_cparams(("arbitrary",)),
        name="gla_sample",
    )(t3(gq), t3(gk), t3(la), gv, gr, gla_norm[None], s0)


def _fox_sample_kernel(pt_ref, *refs, pp, ds):
    del pt_ref
    ck = refs[0:pp]
    cv = refs[pp:2 * pp]
    cl = refs[2 * pp:3 * pp]
    q_ref, kn_ref, vn_ref, lfn_ref, mgt_ref, bm_ref, o_ref, m_s, l_s, acc_s, carry_s, cn_s = refs[3 * pp:]
    nrow = ds * N_FOX_HEADS
    p_id = pl.program_id(1)

    @pl.when(p_id == 0)
    def _():
        m_s[...] = jnp.full_like(m_s, NEG_INF)
        l_s[...] = jnp.zeros_like(l_s)
        acc_s[...] = jnp.zeros_like(acc_s)
        carry_s[...] = jnp.zeros_like(carry_s)
        run = lfn_ref[0, 0:N_FOX_HEADS, :]
        cn_s[0:N_FOX_HEADS, :] = run
        for t in range(1, ds):
            run = run + lfn_ref[0, t * N_FOX_HEADS:(t + 1) * N_FOX_HEADS, :]
            cn_s[t * N_FOX_HEADS:(t + 1) * N_FOX_HEADS, :] = run

    qf = q_ref[0]
    qb = qf.astype(BF16)
    cn = cn_s[...]

    def flash(logits, pv_fn):
        m = m_s[...]
        m_new = jnp.maximum(m, jnp.max(logits, axis=1, keepdims=True))
        alpha = jnp.exp(m - m_new)
        p = jnp.exp(logits - m_new)
        l_s[...] = alpha * l_s[...] + jnp.sum(p, axis=1, keepdims=True)
        acc_s[...] = alpha * acc_s[...] + pv_fn(p)
        m_s[...] = m_new

    for i in range(pp):
        lt = cl[i][0]
        excl = _dot_sel_lhs(lt, mgt_ref[...]) + carry_s[...]
        carry_s[...] = carry_s[...] + jnp.sum(lt, axis=1, keepdims=True)
        s = _dot_nt(qb, ck[i][0].astype(BF16))
        logits = s + cn + jnp.concatenate([excl] * ds, axis=0)
        vb = cv[i][0].astype(BF16)
        flash(logits, lambda p: _dot(p.astype(BF16), vb))

    @pl.when(p_id == pl.num_programs(1) - 1)
    def _():
        rowi = lax.broadcasted_iota(I32, (nrow, 1), 0)
        for sp in range(ds):
            kn = kn_ref[0, sp:sp + 1, :]
            vn = vn_ref[0, sp:sp + 1, :]
            s = jnp.sum(qf * kn, axis=1, keepdims=True)
            cn_key = jnp.concatenate([cn_s[sp * N_FOX_HEADS:(sp + 1) * N_FOX_HEADS, :]] * ds, axis=0)
            logits = jnp.where(rowi >= sp * N_FOX_HEADS, s + cn - cn_key, NEG_INF)
            flash(logits, lambda p: p * vn)
        res = (acc_s[...] / l_s[...]) * bm_ref[...]
        for t in range(ds):
            o_ref[0, t:t + 1, :] = jnp.sum(res[t * N_FOX_HEADS:(t + 1) * N_FOX_HEADS, :], axis=0, keepdims=True)


def _fox_sample(fq, fk, fv, lf, cache_k, cache_v, cache_logf, page_table, pp):
    db, ds, _ = fq.shape
    n_pool = cache_k.shape[0]
    n_pages = page_table.shape[1]
    nrow = ds * N_FOX_HEADS
    ck = cache_k.reshape(n_pool, PAGE_SIZE, FOX_WIDTH)
    cv = cache_v.reshape(n_pool, PAGE_SIZE, FOX_WIDTH)
    clt = cache_logf.reshape(n_pool, PAGE_SIZE, N_FOX_HEADS).transpose(0, 2, 1)
    hmask = (np.arange(FOX_WIDTH)[None, :] // FOX_HEAD_DIM == np.arange(N_FOX_HEADS)[:, None]).astype(np.float32)
    bm = jnp.asarray(np.tile(hmask, (ds, 1)))
    qbd = (fq[:, :, None, :] * jnp.asarray(hmask)[None, None]).reshape(db, nrow, FOX_WIDTH)
    lfn = lf.reshape(db, nrow, 1)
    mgt = jnp.asarray(np.tril(np.ones((PAGE_SIZE, PAGE_SIZE), np.float32), -1), BF16)

    def page_spec(i, shape):
        return pl.BlockSpec(shape, lambda b, p, pt: (pt[b, n_pages - 1 - (p * pp + i)], 0, 0))

    per_b = lambda shape: pl.BlockSpec(shape, lambda b, p, pt: (b, 0, 0))
    in_specs = ([page_spec(i, (1, PAGE_SIZE, FOX_WIDTH)) for i in range(pp)]
                + [page_spec(i, (1, PAGE_SIZE, FOX_WIDTH)) for i in range(pp)]
                + [page_spec(i, (1, N_FOX_HEADS, PAGE_SIZE)) for i in range(pp)]
                + [per_b((1, nrow, FOX_WIDTH)), per_b((1, ds, FOX_WIDTH)), per_b((1, ds, FOX_WIDTH)),
                   per_b((1, nrow, 1)),
                   pl.BlockSpec((PAGE_SIZE, PAGE_SIZE), lambda b, p, pt: (0, 0)),
                   pl.BlockSpec((nrow, FOX_WIDTH), lambda b, p, pt: (0, 0))])
    return pl.pallas_call(
        functools.partial(_fox_sample_kernel, pp=pp, ds=ds),
        grid_spec=pltpu.PrefetchScalarGridSpec(
            num_scalar_prefetch=1,
            grid=(db, n_pages // pp),
            in_specs=in_specs,
            out_specs=per_b((1, ds, FOX_WIDTH)),
            scratch_shapes=[pltpu.VMEM((nrow, 1), F32), pltpu.VMEM((nrow, 1), F32),
                            pltpu.VMEM((nrow, FOX_WIDTH), F32), pltpu.VMEM((N_FOX_HEADS, 1), F32),
                            pltpu.VMEM((nrow, 1), F32)]),
        out_shape=jax.ShapeDtypeStruct((db, ds, FOX_WIDTH), F32),
        compiler_params=_cparams(("arbitrary", "arbitrary")),
        name="fox_sample",
    )(page_table, *([ck] * pp), *([cv] * pp), *([clt] * pp), qbd, fk, fv, lfn, mgt, bm)


def _topk_rows(s, k, keys=None):
    if keys is None:
        keys = lax.broadcasted_iota(I32, s.shape, 0).astype(F32)
    vals, idxs = [], []
    for _ in range(k):
        m = jnp.max(s, axis=0, keepdims=True)
        i = jnp.min(jnp.where(s == m, keys, 1e9), axis=0, keepdims=True)
        s = jnp.where(keys == i, -jnp.inf, s)
        vals.append(m)
        idxs.append(i)
    return jnp.concatenate(vals, axis=0), jnp.concatenate(idxs, axis=0).astype(I32)


def _cand_slabs():
    kk = PEER_TOPK
    slabs = [((0, 1), (0, 8)), ((0, 1), (8, 16)), ((1, 2), (0, 8)), ((2, 3), (0, 8)), ((3, 4), (0, 8)),
             ((4, 5), (0, 8)), ((8, 16), (0, 1)), ((0, 8), (1, 2)), ((0, 8), (0, 1))]
    keys, mask, seen = [], [], set()
    for (a0, a1), (b0, b1) in slabs:
        for a in range(a0, a1):
            for b in range(b0, b1):
                keys.append(a * kk + b)
                mask.append(0.0 if (a, b) not in seen else -np.inf)
                seen.add((a, b))
    assert all((a, b) in seen for a in range(kk) for b in range(kk) if (a + 1) * (b + 1) <= kk)
    return slabs, np.asarray(keys, np.float32), np.asarray(mask, np.float32)


def _route_kernel(x_ref, o_ref, g_ref, wo_ref, nf_ref, wq_ref, kbt_ref, ckey_ref, cmask_ref,
                  h_ref, xn_ref, idx_ref, gate_ref):
    kk = PEER_TOPK
    slabs = _cand_slabs()[0]
    h = x_ref[...] + _dot(o_ref[...], wo_ref[0:FOX_WIDTH, :]) + _dot(g_ref[...], wo_ref[FOX_WIDTH:, :])
    h_ref[...] = h
    xn = (h * lax.rsqrt(jnp.mean(h * h, axis=-1, keepdims=True) + RMS_EPS)) * nf_ref[...]
    xn_ref[...] = xn
    q = _dot(xn.astype(BF16), wq_ref[...]).astype(BF16)
    st = _dot_nt(kbt_ref[...], q)
    tm = st.shape[1]
    for hd in range(PEER_HEADS):
        base = hd * 2 * PEER_N_KEYS
        ts0, ti0 = _topk_rows(st[base:base + PEER_N_KEYS, :], kk)
        ts1, ti1 = _topk_rows(st[base + PEER_N_KEYS:base + 2 * PEER_N_KEYS, :], kk)
        cand = jnp.concatenate([ts0[a0:a1, :] + ts1[b0:b1, :] for (a0, a1), (b0, b1) in slabs], axis=0)
        score, sel = _topk_rows(cand + cmask_ref[...], kk, ckey_ref[...])
        sa = sel >> 4
        sb = sel & (kk - 1)
        i1 = jnp.zeros((kk, tm), I32)
        i2 = jnp.zeros((kk, tm), I32)
        for a in range(kk):
            i1 = jnp.where(sa == a, ti0[a:a + 1, :], i1)
            i2 = jnp.where(sb == a, ti1[a:a + 1, :], i2)
        e = jnp.exp(score - jnp.max(score, axis=0, keepdims=True))
        gate = e / jnp.sum(e, axis=0, keepdims=True)
        idx_ref[hd * kk:(hd + 1) * kk, :] = i1 * PEER_N_KEYS + i2
        gate_ref[hd * kk:(hd + 1) * kk, :] = gate


def _route(x, o, g, w_out, norm_ffn, peer_wq, peer_sub_keys, tm):
    t = x.shape[0]
    nb = PEER_HEADS * 2
    eye = jnp.eye(nb, dtype=F32)
    kbt = (peer_sub_keys.reshape(nb, PEER_N_KEYS, 1, PEER_HALF) * eye[:, None, :, None]).reshape(
        nb * PEER_N_KEYS, nb * PEER_HALF).astype(BF16)
    _, keys, mask = _cand_slabs()
    ckey = jnp.asarray(np.broadcast_to(keys[:, None], (keys.size, tm)))
    cmask = jnp.asarray(np.broadcast_to(mask[:, None], (mask.size, tm)))
    tok = lambda width: pl.BlockSpec((tm, width), lambda i: (i, 0))
    colb = pl.BlockSpec((PEER_SLOTS, tm), lambda i: (0, i))
    return pl.pallas_call(
        _route_kernel,
        grid=(t // tm,),
        in_specs=[tok(D_MODEL), tok(FOX_WIDTH), tok(GLA_V_WIDTH), _full((D_MODEL, D_MODEL)), _full((1, D_MODEL)),
                  _full((D_MODEL, D_MODEL)), _full(kbt.shape), _full(ckey.shape), _full(cmask.shape)],
        out_specs=[tok(D_MODEL), tok(D_MODEL), colb, colb],
        out_shape=[jax.ShapeDtypeStruct((t, D_MODEL), F32), jax.ShapeDtypeStruct((t, D_MODEL), F32),
                   jax.ShapeDtypeStruct((PEER_SLOTS, t), I32), jax.ShapeDtypeStruct((PEER_SLOTS, t), F32)],
        compiler_params=_cparams(("arbitrary",)),
        name="route",
    )(x, o, g, w_out.astype(BF16), norm_ffn[None], peer_wq.astype(BF16), kbt, ckey, cmask)


def _pack_table(tab):
    e = tab.shape[0]
    b = lax.bitcast_convert_type(tab.astype(BF16), jnp.uint16).astype(jnp.uint32)
    w = b[:, :D_MODEL // 2] | (b[:, D_MODEL // 2:] << 16)
    return lax.bitcast_convert_type(w, I32).reshape(e * 4, LANES)


def _unpack(tile):
    lo = lax.bitcast_convert_type(tile << 16, F32)
    hi = lax.bitcast_convert_type(tile & jnp.int32(-65536), F32)
    return lo, hi


def _gelu(x):
    return 0.5 * x * (1.0 + lax.erf(x * (2.0 ** -0.5)))


PEER_TOKEN_UNROLL = 2
PEER_U_CHAINS = 8
TILE_ROWS_BF16 = 16


def _peer_u_kernel(off_ref, x_ref, par_ref, gate_ref, tab_ref, w0_ref, w1_ref):
    tb = x_ref.shape[0]
    sub = lax.broadcasted_iota(I32, (SUBLANES, LANES), 0)
    lane = lax.broadcasted_iota(I32, (SUBLANES, LANES), 1)
    low = sub < 4

    def token(t, _):
        x8 = x_ref[t]
        xr = pltpu.roll(x8, 4, 0)
        xlo = jnp.where(low, x8, xr)
        xhi = jnp.where(low, xr, x8)
        parts = [jnp.zeros((SUBLANES, LANES), F32) for _ in range(PEER_U_CHAINS)]
        for j in range(PEER_SLOTS):
            off = pl.multiple_of(off_ref[t, j], SUBLANES)
            lo, hi = _unpack(tab_ref[pl.ds(off, SUBLANES), :])
            r = jnp.sum(lo * xlo + hi * xhi, axis=1, keepdims=True)
            parts[j % PEER_U_CHAINS] = jnp.where(lane == j, r, parts[j % PEER_U_CHAINS])
        r8 = parts[0]
        for p in parts[1:]:
            r8 = r8 + p
        a0 = r8[0:1] + r8[1:2] + r8[2:3] + r8[3:4]
        a1 = r8[4:5] + r8[5:6] + r8[6:7] + r8[7:8]
        odd = par_ref[pl.ds(t, 1), :] == 1
        w = gate_ref[pl.ds(t, 1), :] * _gelu(jnp.where(odd, a1, a0))
        w0_ref[pl.ds(t, 1), :] = jnp.where(odd, 0.0, w)
        w1_ref[pl.ds(t, 1), :] = jnp.where(odd, w, 0.0)
        return 0

    lax.fori_loop(0, tb, token, 0, unroll=PEER_TOKEN_UNROLL)


def _peer_v_kernel(off_ref, w0_ref, w1_ref, h_ref, e0_ref, e1_ref, m8_ref, tab_ref, y_ref, wexp):
    tb = h_ref.shape[0]
    wexp[...] = _dot_sel_lhs(w0_ref[...], e0_ref[...]) + _dot_sel_lhs(w1_ref[...], e1_ref[...])

    def token(t, _):
        wm = wexp[pl.ds(t, 1), :] * m8_ref[...]
        hi = wm.astype(BF16)
        lo = (wm - hi.astype(F32)).astype(BF16)
        tiles = []
        for j in range(PEER_SLOTS):
            off = pl.multiple_of(off_ref[t, j], SUBLANES)
            tiles.append(pltpu.bitcast(tab_ref[pl.ds(off, SUBLANES), :], BF16))
        r = _dot(jnp.concatenate([hi, lo], axis=0), jnp.concatenate(tiles, axis=0))
        y_ref[t] = h_ref[t] + (r[0:SUBLANES] + r[SUBLANES:])
        return 0

    lax.fori_loop(0, tb, token, 0, unroll=PEER_TOKEN_UNROLL)


def _peer_v_consts():
    rho = np.arange(TILE_ROWS_BF16)
    k = PEER_SLOTS * TILE_ROWS_BF16
    slot = np.arange(k) // TILE_ROWS_BF16
    r_k = np.arange(k) % TILE_ROWS_BF16
    sel = (np.arange(PEER_SLOTS)[:, None] == slot[None, :])
    e0 = (sel & (r_k[None, :] < SUBLANES)).astype(np.float32)
    e1 = (sel & (r_k[None, :] >= SUBLANES)).astype(np.float32)
    chunk = (rho // 2) % 4
    half = rho % 2
    out_row = half * 4 + chunk
    m8 = (np.arange(SUBLANES)[:, None] == out_row[r_k][None, :]).astype(np.float32)
    return jnp.asarray(e0, BF16), jnp.asarray(e1, BF16), jnp.asarray(m8)


def _peer_gather(h, xn, idx_t, gate_t, tab_u, tab_v, tb):
    t = h.shape[0]
    idx = idx_t.T
    gate = gate_t.T
    off = (idx >> 1) << 3
    par = idx & 1
    x8 = xn.reshape(t, SUBLANES, LANES)
    h8 = h.reshape(t, SUBLANES, LANES)
    e0, e1, m8 = _peer_v_consts()
    smem = pl.BlockSpec((tb, PEER_SLOTS), lambda i: (i, 0), memory_space=pltpu.SMEM)
    tok3 = pl.BlockSpec((tb, SUBLANES, LANES), lambda i: (i, 0, 0))
    tok2 = pl.BlockSpec((tb, PEER_SLOTS), lambda i: (i, 0))
    tabs = pl.BlockSpec(tab_u.shape, lambda i: (0, 0), pipeline_mode=pl.Buffered(1))
    w0, w1 = pl.pallas_call(
        _peer_u_kernel,
        grid=(t // tb,),
        in_specs=[smem, tok3, tok2, tok2, tabs],
        out_specs=[tok2, tok2],
        out_shape=[jax.ShapeDtypeStruct((t, PEER_SLOTS), F32)] * 2,
        compiler_params=_cparams(("arbitrary",)),
        name="peer_u",
    )(off, x8, par, gate, tab_u)
    y8 = pl.pallas_call(
        _peer_v_kernel,
        grid=(t // tb,),
        in_specs=[smem, tok2, tok2, tok3, _full(e0.shape), _full(e1.shape), _full(m8.shape), tabs],
        out_specs=tok3,
        out_shape=jax.ShapeDtypeStruct((t, SUBLANES, LANES), F32),
        scratch_shapes=[pltpu.VMEM((tb, PEER_SLOTS * TILE_ROWS_BF16), F32)],
        compiler_params=_cparams(("arbitrary",)),
        name="peer_v",
    )(off, w0, w1, h8, e0, e1, m8, tab_v)
    return y8.reshape(t, D_MODEL)


def _pick(n, pref):
    t = min(n, pref)
    while n % t:
        t //= 2
    return t


def kernel(x_prompt, x_sample, cache_k, cache_v, cache_logf, state_gla, page_table, norm_mix, w_in, b_forget,
           q_norm, k_norm, w_gate_up, b_gate, gla_norm, w_out, norm_ffn, peer_wq, peer_sub_keys, peer_u, peer_v):
    depth = w_in.shape[0]
    assert depth == 1, "single-layer step"
    l = 0
    b, s, _ = x_prompt.shape
    db, ds, _ = x_sample.shape
    tab_u = _pack_table(peer_u[l])
    tab_v = _pack_table(peer_v[l])

    wts_p = _proj_weights(w_in[l], b_forget[l], q_norm[l], k_norm[l], w_gate_up[l], b_gate[l], prompt=True)
    (qaug, kaug, fk_p, fv_p, fvb, lf_p, gq, gk, gvb, la, gr) = _proj_prompt(x_prompt, norm_mix[l], wts_p, _pick(s, 256))
    fo_p = _fox_prompt(qaug, kaug, fvb, _pick(s, 512))
    g_p, sn_p = _gla_prompt(gq, gk, la, gvb, gr, gla_norm[l], _pick(s, 256))
    tp = b * s
    h_p, xn_p, idx_p, gate_p = _route(x_prompt.reshape(tp, D_MODEL), fo_p.reshape(tp, FOX_WIDTH),
                                      g_p.reshape(tp, GLA_V_WIDTH), w_out[l], norm_ffn[l], peer_wq[l],
                                      peer_sub_keys[l], _pick(tp, 256))
    y_p = _peer_gather(h_p, xn_p, idx_p, gate_p, tab_u, tab_v, _pick(tp, 64)).reshape(b, s, D_MODEL)

    ts = db * ds
    wts_s = _proj_weights(w_in[l], b_forget[l], q_norm[l], k_norm[l], w_gate_up[l], b_gate[l], prompt=False)
    (fq_s, fk_s, fv_s, lf_s, gq_s, gk_s, gvb_s, la_s, gr_s) = _proj_sample(
        x_sample.reshape(1, ts, D_MODEL), norm_mix[l], wts_s, _pick(ts, 256))
    r3 = lambda a: a.reshape(db, ds, a.shape[-1])
    lf_s8 = r3(lf_s)[:, :, :N_FOX_HEADS]
    n_pages = page_table.shape[1]
    fo_s = _fox_sample(r3(fq_s), r3(fk_s), r3(fv_s), lf_s8, cache_k[l], cache_v[l], cache_logf[l], page_table,
                       _pick(n_pages, 4))
    g_s, sn_s = _gla_sample(r3(gq_s), r3(gk_s), r3(la_s), r3(gvb_s), r3(gr_s), gla_norm[l], state_gla[l])
    h_s, xn_s, idx_s, gate_s = _route(x_sample.reshape(ts, D_MODEL), fo_s.reshape(ts, FOX_WIDTH).astype(BF16),
                                      g_s.reshape(ts, GLA_V_WIDTH).astype(BF16), w_out[l], norm_ffn[l],
                                      peer_wq[l], peer_sub_keys[l], _pick(ts, 256))
    y_s = _peer_gather(h_s, xn_s, idx_s, gate_s, tab_u, tab_v, _pick(ts, 64)).reshape(db, ds, D_MODEL)

    hd = (N_FOX_HEADS, FOX_HEAD_DIM)
    return (y_p, y_s,
            fk_p.reshape(1, b, s, *hd), fv_p.reshape(1, b, s, *hd), lf_p[None, :, :, :N_FOX_HEADS], sn_p[None],
            r3(fk_s).reshape(1, db, ds, *hd), r3(fv_s).reshape(1, db, ds, *hd), lf_s8[None], sn_s[None])
```

```python
import functools

import numpy as np
import jax
import jax.numpy as jnp
from jax import lax
from jax.experimental import pallas as pl
from jax.experimental.pallas import tpu as pltpu

F32 = jnp.float32
BF16 = jnp.bfloat16
I32 = jnp.int32

D_MODEL = 1024
N_FOX_HEADS = 8
FOX_HEAD_DIM = 64
N_GLA_HEADS = 4
GLA_DK = 64
GLA_DV = 128
GLA_GATE_RANK = 16
GLA_GATE_TEMP = 16.0
GLA_CHUNK = 64
PEER_HEADS = 8
PEER_N_KEYS = 128
PEER_TOPK = 16
PEER_HALF = 64
PAGE_SIZE = 128
RMS_EPS = 1e-6
NEG_INF = -1e30

FOX_WIDTH = N_FOX_HEADS * FOX_HEAD_DIM
GLA_QK_WIDTH = N_GLA_HEADS * GLA_DK
GLA_V_WIDTH = N_GLA_HEADS * GLA_DV
FOX_SCALE = FOX_HEAD_DIM ** -0.5
GLA_SCALE = GLA_DK ** -0.5
LOG2E = 1.4426950408889634
IN_SIZES = [FOX_WIDTH, FOX_WIDTH, FOX_WIDTH, N_FOX_HEADS,
            GLA_QK_WIDTH, GLA_QK_WIDTH, GLA_V_WIDTH, GLA_GATE_RANK, GLA_V_WIDTH]
IN_OFFS = [int(c) for c in np.cumsum([0] + IN_SIZES)]
PEER_SLOTS = PEER_HEADS * PEER_TOPK

LANES = 128
SUBLANES = 8
VMEM_LIMIT = 56 * 1024 * 1024


def _cparams(sem, vmem=VMEM_LIMIT):
    return pltpu.CompilerParams(dimension_semantics=sem, vmem_limit_bytes=vmem)


def _dot(a, b):
    return jnp.dot(a, b, preferred_element_type=F32)


def _dot_nt(a, b):
    return lax.dot_general(a, b, (((1,), (1,)), ((), ())), preferred_element_type=F32)


def _split3(x):
    hi = x.astype(BF16)
    r = x - hi.astype(F32)
    mid = r.astype(BF16)
    lo = (r - mid.astype(F32)).astype(BF16)
    return hi, mid, lo


def _dot_sel_rhs(sel_bf16, x):
    hi, mid, lo = _split3(x)
    return _dot(sel_bf16, hi) + _dot(sel_bf16, mid) + _dot(sel_bf16, lo)


def _dot_sel_lhs(x, sel_bf16):
    hi, mid, lo = _split3(x)
    return _dot(hi, sel_bf16) + _dot(mid, sel_bf16) + _dot(lo, sel_bf16)


def _log_sigmoid(x):
    return -(jnp.maximum(-x, 0.0) + jnp.log1p(jnp.exp(-jnp.abs(x))))


def _group_rms(raw, a_ref):
    sq = raw * raw
    hi = sq.astype(BF16)
    lo = (sq - hi.astype(F32)).astype(BF16)
    return _dot(hi, a_ref[...]) + _dot(lo, a_ref[...])


P_QA, P_KA, P_FK, P_FV, P_GQ, P_GK, P_GV, P_GR, P_SM = 0, 1024, 2048, 2560, 3072, 3328, 3584, 4096, 4608
P_NC = 4736
S_FQ, S_FK, S_FV, S_GQ, S_GK, S_GV, S_GR, S_SM = 0, 512, 1024, 1536, 1792, 2048, 2560, 3072
S_NC = 3200


def _proj_common(xn, w_ref, o_fk, o_fv, o_gq, o_gk, o_gv, o_gr, sm, a_ref, knc_ref, wg_ref, bg_ref,
                 fk_ref, fv_ref, gq_ref, gk_ref, gvb_ref, la_ref, gr_ref):
    fk_raw = _dot(xn, w_ref[:, o_fk:o_fk + 512])
    fk_ref[0] = (fk_raw * lax.rsqrt(_group_rms(fk_raw, a_ref) + RMS_EPS)) * knc_ref[...]
    fv = _dot(xn, w_ref[:, o_fv:o_fv + 512])
    fv_ref[0] = fv
    gq_ref[0] = _dot(xn, w_ref[:, o_gq:o_gq + 256]) * GLA_SCALE
    gk_ref[0] = _dot(xn, w_ref[:, o_gk:o_gk + 256])
    gvb_ref[0] = _dot(xn, w_ref[:, o_gv:o_gv + 512]).astype(BF16)
    gr_ref[0] = _dot(xn, w_ref[:, o_gr:o_gr + 512])
    la_ref[0] = _log_sigmoid(_dot(sm.astype(BF16), wg_ref[...]) + bg_ref[...]) * (1.0 / GLA_GATE_TEMP)
    return fv


def _xnorm(x_ref, nm_ref):
    x = x_ref[0]
    ms = jnp.mean(x * x, axis=-1, keepdims=True)
    return ((x * lax.rsqrt(ms + RMS_EPS)) * nm_ref[...]).astype(BF16)


def _proj_prompt_kernel(x_ref, nm_ref, w_ref, bf_ref, qn_ref, kn_ref, knc_ref, a_ref, wg_ref, bg_ref, tri_ref,
                        qaug_ref, kaug_ref, fk_ref, fv_ref, fvb_ref, lf_ref, gq_ref, gk_ref, gvb_ref, la_ref,
                        gr_ref, carry):
    tm = x_ref.shape[1]

    @pl.when(pl.program_id(1) == 0)
    def _():
        carry[...] = jnp.zeros_like(carry)

    xn = _xnorm(x_ref, nm_ref)
    sm = _dot(xn, w_ref[:, P_SM:P_SM + LANES])
    lf = _log_sigmoid(sm + bf_ref[...])
    lf_ref[0] = lf
    cum = _dot_sel_rhs(tri_ref[...], lf) + carry[...]
    carry[...] = cum[tm - 1:tm, :]

    fv = _proj_common(xn, w_ref, P_FK, P_FV, P_GQ, P_GK, P_GV, P_GR, sm, a_ref, knc_ref, wg_ref, bg_ref,
                      fk_ref, fv_ref, gq_ref, gk_ref, gvb_ref, la_ref, gr_ref)
    fvb_ref[0] = fv.astype(BF16)

    qa_all = _dot(xn, w_ref[:, P_QA:P_QA + 1024])
    ka_all = _dot(xn, w_ref[:, P_KA:P_KA + 1024])
    lane = lax.broadcasted_iota(I32, (tm, LANES), 1)
    inv = 1.0 / FOX_HEAD_DIM
    for h in range(N_FOX_HEADS):
        sl = slice(h * LANES, (h + 1) * LANES)
        qa = qa_all[:, sl]
        qn = (qa * lax.rsqrt(jnp.sum(qa * qa, axis=-1, keepdims=True) * inv + RMS_EPS)) * qn_ref[...]
        qn = jnp.where(lane >= 64, jnp.where(lane < 67, -1.0, 0.0), qn)
        qaug_ref[0, :, sl] = qn.astype(BF16)
        ka = ka_all[:, sl]
        kn = (ka * lax.rsqrt(jnp.sum(ka * ka, axis=-1, keepdims=True) * inv + RMS_EPS)) * kn_ref[...]
        c = cum[:, h:h + 1] * LOG2E
        c_hi = c.astype(BF16).astype(F32)
        r = c - c_hi
        c_mid = r.astype(BF16).astype(F32)
        c_lo = r - c_mid
        kn = jnp.where(lane == 64, c_hi, jnp.where(lane == 65, c_mid, jnp.where(lane == 66, c_lo, kn)))
        kaug_ref[0, :, sl] = kn.astype(BF16)


def _proj_sample_kernel(x_ref, nm_ref, w_ref, bf_ref, qnc_ref, knc_ref, a_ref, wg_ref, bg_ref,
                        fq_ref, fk_ref, fv_ref, lf_ref, gq_ref, gk_ref, gvb_ref, la_ref, gr_ref):
    xn = _xnorm(x_ref, nm_ref)
    sm = _dot(xn, w_ref[:, S_SM:S_SM + LANES])
    lf_ref[0] = _log_sigmoid(sm + bf_ref[...])
    fq_raw = _dot(xn, w_ref[:, S_FQ:S_FQ + 512])
    fq_ref[0] = (fq_raw * lax.rsqrt(_group_rms(fq_raw, a_ref) + RMS_EPS)) * qnc_ref[...]
    _proj_common(xn, w_ref, S_FK, S_FV, S_GQ, S_GK, S_GV, S_GR, sm, a_ref, knc_ref, wg_ref, bg_ref,
                 fk_ref, fv_ref, gq_ref, gk_ref, gvb_ref, la_ref, gr_ref)


def _full(shape):
    nd = len(shape)
    return pl.BlockSpec(shape, lambda *_: (0,) * nd)


def _proj_weights(w_in, b_forget, q_norm, k_norm, w_gate_up, b_gate, prompt):
    o = IN_OFFS
    wq, wk, wv, wf = (w_in[:, o[i]:o[i + 1]] for i in range(4))
    wgq, wgk, wgv, wga, wgr = (w_in[:, o[i]:o[i + 1]] for i in range(4, 9))
    small = jnp.concatenate([wf, wga, jnp.zeros((D_MODEL, LANES - 24), F32)], axis=1)

    def expand(w):
        w3 = w.reshape(D_MODEL, N_FOX_HEADS, FOX_HEAD_DIM)
        return jnp.concatenate([w3, jnp.zeros_like(w3)], axis=-1).reshape(D_MODEL, N_FOX_HEADS * LANES)

    if prompt:
        cols = [expand(wq), expand(wk), wk, wv, wgq, wgk, wgv, wgr, small]
    else:
        cols = [wq, wk, wv, wgq, wgk, wgv, wgr, small]
    w = jnp.concatenate(cols, axis=1).astype(BF16)
    bf = jnp.concatenate([b_forget, jnp.zeros((LANES - N_FOX_HEADS,), F32)])[None]
    zpad = jnp.zeros((LANES - FOX_HEAD_DIM,), F32)
    qn_pad = jnp.concatenate([q_norm * (FOX_SCALE * LOG2E), zpad])[None]
    kn_pad = jnp.concatenate([k_norm, zpad])[None]
    qnc = jnp.tile(q_norm * FOX_SCALE, N_FOX_HEADS)[None]
    knc = jnp.tile(k_norm, N_FOX_HEADS)[None]
    g = np.arange(FOX_WIDTH) // FOX_HEAD_DIM
    a = jnp.asarray((g[:, None] == g[None, :]).astype(np.float32) / FOX_HEAD_DIM, BF16)
    wg = jnp.zeros((LANES, GLA_QK_WIDTH), F32).at[N_FOX_HEADS:N_FOX_HEADS + GLA_GATE_RANK].set(w_gate_up).astype(BF16)
    return w, bf, qn_pad, kn_pad, qnc, knc, a, wg, b_gate[None]


def _proj_prompt(x, norm_mix, wts, tm):
    b, s, _ = x.shape
    w, bf, qn_pad, kn_pad, _, knc, a, wg, bg = wts
    tri = jnp.asarray(np.tril(np.ones((tm, tm), np.float32)), BF16)
    tok = lambda width: pl.BlockSpec((1, tm, width), lambda i, j: (i, j, 0))
    shp = lambda width, dt: jax.ShapeDtypeStruct((b, s, width), dt)
    return pl.pallas_call(
        _proj_prompt_kernel,
        grid=(b, s // tm),
        in_specs=[tok(D_MODEL), _full((1, D_MODEL)), _full(w.shape), _full((1, LANES)), _full((1, LANES)),
                  _full((1, LANES)), _full((1, FOX_WIDTH)), _full(a.shape), _full(wg.shape),
                  _full((1, GLA_QK_WIDTH)), _full((tm, tm))],
        out_specs=[tok(1024), tok(1024), tok(512), tok(512), tok(512), tok(LANES), tok(256), tok(256), tok(512),
                   tok(256), tok(512)],
        out_shape=[shp(1024, BF16), shp(1024, BF16), shp(512, F32), shp(512, F32), shp(512, BF16), shp(LANES, F32),
                   shp(256, F32), shp(256, F32), shp(512, BF16), shp(256, F32), shp(512, F32)],
        scratch_shapes=[pltpu.VMEM((1, LANES), F32)],
        compiler_params=_cparams(("arbitrary", "arbitrary")),
        name="proj_prompt",
    )(x, norm_mix[None], w, bf, qn_pad, kn_pad, knc, a, wg, bg, tri)


def _proj_sample(x, norm_mix, wts, tm):
    b, s, _ = x.shape
    w, bf, _, _, qnc, knc, a, wg, bg = wts
    tok = lambda width: pl.BlockSpec((1, tm, width), lambda i, j: (i, j, 0))
    shp = lambda width, dt: jax.ShapeDtypeStruct((b, s, width), dt)
    return pl.pallas_call(
        _proj_sample_kernel,
        grid=(b, s // tm),
        in_specs=[tok(D_MODEL), _full((1, D_MODEL)), _full(w.shape), _full((1, LANES)), _full((1, FOX_WIDTH)),
                  _full((1, FOX_WIDTH)), _full(a.shape), _full(wg.shape), _full((1, GLA_QK_WIDTH))],
        out_specs=[tok(512), tok(512), tok(512), tok(LANES), tok(256), tok(256), tok(512), tok(256), tok(512)],
        out_shape=[shp(512, F32), shp(512, F32), shp(512, F32), shp(LANES, F32), shp(256, F32), shp(256, F32),
                   shp(512, BF16), shp(256, F32), shp(512, F32)],
        compiler_params=_cparams(("arbitrary", "arbitrary")),
        name="proj_sample",
    )(x, norm_mix[None], w, bf, qnc, knc, a, wg, bg)


FOX_HEADS_PER_STEP = 4


def _fox_prompt_kernel(kaug_ref, qaug_ref, vt_ref, o_ref):
    tq = qaug_ref.shape[1]
    tk = vt_ref.shape[4]
    nh = vt_ref.shape[1]
    qi = pl.program_id(2)
    qs = [qaug_ref[0, :, g * LANES:(g + 1) * LANES] for g in range(nh)]

    def scores(kj, g):
        k = kaug_ref[0, pl.ds(pl.multiple_of(kj * tk, tk), tk), g * LANES:(g + 1) * LANES]
        return _dot_nt(k, qs[g])

    def softmax_step(st, carry):
        m, l, acc = carry
        m_new = jnp.maximum(m, jnp.max(st, axis=0, keepdims=True))
        alpha = jnp.exp2(m - m_new)
        p = jnp.exp2(st - m_new)
        return m_new, alpha * l + jnp.sum(p, axis=0, keepdims=True), alpha * acc, p.astype(BF16)

    def block(kj, carry, mask):
        sts = [scores(kj, g) for g in range(nh)]
        if mask is not None:
            sts = [jnp.where(mask, st, NEG_INF) for st in sts]
        mid = [softmax_step(sts[g], carry[g]) for g in range(nh)]
        return tuple((m, l, acc + _dot(vt_ref[0, g, kj], p)) for g, (m, l, acc, p) in enumerate(mid))

    init = tuple((jnp.full((1, tq), NEG_INF, F32), jnp.zeros((1, tq), F32), jnp.zeros((FOX_HEAD_DIM, tq), F32))
                 for _ in range(nh))
    carry = lax.fori_loop(0, qi, lambda kj, c: block(kj, c, None), init)
    row = lax.broadcasted_iota(I32, (tk, tq), 0)
    col = lax.broadcasted_iota(I32, (tk, tq), 1)
    carry = block(qi, carry, row <= col)
    for g in range(nh):
        _, l, acc = carry[g]
        o_ref[0, g] = (acc / l).astype(BF16)


def _fox_prompt(qaug, kaug, fvb, t):
    b, s, _ = qaug.shape
    n = s // t
    nh = FOX_HEADS_PER_STEP
    vt = fvb.reshape(b, n, t, N_FOX_HEADS, FOX_HEAD_DIM).transpose(0, 3, 1, 4, 2)
    ot = pl.pallas_call(
        _fox_prompt_kernel,
        grid=(b, N_FOX_HEADS // nh, n),
        in_specs=[pl.BlockSpec((1, s, nh * LANES), lambda i, h, q: (i, 0, h), pipeline_mode=pl.Buffered(1)),
                  pl.BlockSpec((1, t, nh * LANES), lambda i, h, q: (i, q, h)),
                  pl.BlockSpec((1, nh, n, FOX_HEAD_DIM, t), lambda i, h, q: (i, h, 0, 0, 0),
                               pipeline_mode=pl.Buffered(1))],
        out_specs=pl.BlockSpec((1, nh, FOX_HEAD_DIM, t), lambda i, h, q: (i, h, 0, q)),
        out_shape=jax.ShapeDtypeStruct((b, N_FOX_HEADS, FOX_HEAD_DIM, s), BF16),
        compiler_params=_cparams(("arbitrary", "arbitrary", "arbitrary")),
        name="fox_prompt",
    )(kaug, qaug, vt)
    return ot.transpose(0, 3, 1, 2).reshape(b, s, FOX_WIDTH)


def _gla_gate_out(o, gr, gn_ref):
    on = (o * lax.rsqrt(jnp.mean(o * o, axis=-1, keepdims=True) + RMS_EPS)) * gn_ref[...]
    return on * (gr * jax.nn.sigmoid(gr))


def _gla_prompt_kernel(gq_ref, gk_ref, la_ref, gkt_ref, lat_ref, gv_ref, gr_ref, gn_ref, tril_ref, triu_ref,
                       g_ref, sn_ref, state):
    c = GLA_CHUNK
    nch = gq_ref.shape[1] // c

    @pl.when(pl.program_id(1) == 0)
    def _():
        state[...] = jnp.zeros_like(state)

    lane = lax.broadcasted_iota(I32, (c, GLA_QK_WIDTH), 1)
    row = lax.broadcasted_iota(I32, (c, c), 0)
    col = lax.broadcasted_iota(I32, (c, c), 1)
    causal = col <= row
    for ci in range(nch):
        rs = slice(ci * c, (ci + 1) * c)
        q = gq_ref[0, rs, :]
        k = gk_ref[0, rs, :]
        v = gv_ref[0, rs, :]
        cum = _dot_sel_rhs(tril_ref[...], la_ref[0, rs, :])
        cum_t = _dot_sel_lhs(lat_ref[0, ci], triu_ref[...])
        last_col = cum_t[:, c - 1:c]
        mid = cum[c // 2 - 1:c // 2, :]
        qs = (q * jnp.exp(cum - mid))
        ks = (k * jnp.exp(mid - cum)).astype(BF16)
        qd = q * jnp.exp(cum)
        kh_t = (gkt_ref[0, ci] * jnp.exp(last_col - cum_t)).astype(BF16)
        st = state[...]
        st_b = st.astype(BF16)
        upd = _dot(kh_t, v)
        new_rows = []
        for h in range(N_GLA_HEADS):
            hm = (lane >= h * GLA_DK) & (lane < (h + 1) * GLA_DK)
            sc = _dot_nt(jnp.where(hm, qs, 0.0).astype(BF16), ks)
            sc = jnp.where(causal, sc, 0.0).astype(BF16)
            vs = slice(h * GLA_DV, (h + 1) * GLA_DV)
            o = _dot(sc, v[:, vs]) + _dot(jnp.where(hm, qd, 0.0).astype(BF16), st_b)
            g_ref[0, rs, vs] = _gla_gate_out(o, gr_ref[0, rs, vs], gn_ref).astype(BF16)
            new_rows.append(upd[h * GLA_DK:(h + 1) * GLA_DK, vs])
        state[...] = jnp.exp(last_col) * st + jnp.concatenate(new_rows, axis=0)

    @pl.when(pl.program_id(1) == pl.num_programs(1) - 1)
    def _():
        sn_ref[0] = state[...]


def _gla_prompt(gq, gk, la, gvb, gr, gla_norm, tg):
    b, s, _ = gq.shape
    c = GLA_CHUNK
    nch = tg // c
    to_t = lambda a: a.reshape(b, s // c, c, GLA_QK_WIDTH).transpose(0, 1, 3, 2)
    tril = jnp.asarray(np.tril(np.ones((c, c), np.float32)), BF16)
    triu = jnp.asarray(np.triu(np.ones((c, c), np.float32)), BF16)
    tok = lambda width: pl.BlockSpec((1, tg, width), lambda i, j: (i, j, 0))
    tr = pl.BlockSpec((1, nch, GLA_QK_WIDTH, c), lambda i, j: (i, j, 0, 0))
    g, sn = pl.pallas_call(
        _gla_prompt_kernel,
        grid=(b, s // tg),
        in_specs=[tok(256), tok(256), tok(256), tr, tr, tok(512), tok(512), _full((1, GLA_DV)), _full((c, c)),
                  _full((c, c))],
        out_specs=[tok(512), pl.BlockSpec((1, GLA_QK_WIDTH, GLA_DV), lambda i, j: (i, 0, 0))],
        out_shape=[jax.ShapeDtypeStruct((b, s, GLA_V_WIDTH), BF16),
                   jax.ShapeDtypeStruct((b, GLA_QK_WIDTH, GLA_DV), F32)],
        scratch_shapes=[pltpu.VMEM((GLA_QK_WIDTH, GLA_DV), F32)],
        compiler_params=_cparams(("arbitrary", "arbitrary")),
        name="gla_prompt",
    )(gq, gk, la, to_t(gk), to_t(la), gvb, gr, gla_norm[None], tril, triu)
    return g, sn.reshape(b, N_GLA_HEADS, GLA_DK, GLA_DV)


def _gla_sample_kernel(qt_ref, kt_ref, lt_ref, v_ref, gr_ref, gn_ref, s0_ref, g_ref, sn_ref):
    ds = v_ref.shape[1]
    for h in range(N_GLA_HEADS):
        st = s0_ref[0, h]
        rs = slice(h * GLA_DK, (h + 1) * GLA_DK)
        vs = slice(h * GLA_DV, (h + 1) * GLA_DV)
        for t in range(ds):
            a = jnp.exp(lt_ref[0, rs, t:t + 1])
            st = a * st + kt_ref[0, rs, t:t + 1] * v_ref[0, t:t + 1, vs].astype(F32)
            o = jnp.sum(qt_ref[0, rs, t:t + 1] * st, axis=0, keepdims=True)
            g_ref[0, t:t + 1, vs] = _gla_gate_out(o, gr_ref[0, t:t + 1, vs], gn_ref)
        sn_ref[0, h] = st


def _gla_sample(gq, gk, la, gv, gr, gla_norm, s0):
    b, ds, _ = gq.shape
    t3 = lambda a: a.transpose(0, 2, 1)
    colb = pl.BlockSpec((1, GLA_QK_WIDTH, ds), lambda i: (i, 0, 0))
    rowb = pl.BlockSpec((1, ds, GLA_V_WIDTH), lambda i: (i, 0, 0))
    stb = pl.BlockSpec((1, N_GLA_HEADS, GLA_DK, GLA_DV), lambda i: (i, 0, 0, 0))
    return pl.pallas_call(
        _gla_sample_kernel,
        grid=(b,),
        in_specs=[colb, colb, colb, rowb, rowb, _full((1, GLA_DV)), stb],
        out_specs=[rowb, stb],
        out_shape=[jax.ShapeDtypeStruct((b, ds, GLA_V_WIDTH), F32),
                   jax.ShapeDtypeStruct((b, N_GLA_HEADS, GLA_DK, GLA_DV), F32)],
        compiler_params=_cparams(("arbitrary",)),
        name="gla_sample",
    )(t3(gq), t3(gk), t3(la), gv, gr, gla_norm[None], s0)


def _fox_sample_kernel(pt_ref, *refs, pp, ds):
    del pt_ref
    ck = refs[0:pp]
    cv = refs[pp:2 * pp]
    cl = refs[2 * pp:3 * pp]
    (q_ref, kn_ref, vn_ref, lfn_ref, hm_ref, suf_ref, same_ref, later_ref, ones_ref,
     o_ref, m_s, l_s, acc_s, carry_s, cn_s) = refs[3 * pp:]
    nh = N_FOX_HEADS
    nrow = ds * nh
    page_rows = PAGE_SIZE * nh
    lf_rows = page_rows // LANES
    p_id = pl.program_id(1)

    @pl.when(p_id == 0)
    def _():
        m_s[...] = jnp.full_like(m_s, NEG_INF)
        l_s[...] = jnp.zeros_like(l_s)
        acc_s[...] = jnp.zeros_like(acc_s)
        carry_s[...] = jnp.zeros_like(carry_s)
        run = lfn_ref[0, 0:nh, :]
        cn_s[0:nh, :] = run
        for t in range(1, ds):
            run = run + lfn_ref[0, t * nh:(t + 1) * nh, :]
            cn_s[t * nh:(t + 1) * nh, :] = run

    qf = q_ref[0]
    qb = qf.astype(BF16)
    cn = cn_s[...]

    def flash(logits, pv_fn):
        m = m_s[...]
        m_new = jnp.maximum(m, jnp.max(logits, axis=1, keepdims=True))
        alpha = jnp.exp(m - m_new)
        p = jnp.exp(logits - m_new)
        l_s[...] = alpha * l_s[...] + jnp.sum(p, axis=1, keepdims=True)
        acc_s[...] = alpha * acc_s[...] + pv_fn(p)
        m_s[...] = m_new

    lf = jnp.concatenate([cl[i][0] for i in range(pp)], axis=0)
    rowtot = _dot_sel_lhs(lf, same_ref[...])
    bias = ((_dot_sel_lhs(lf, suf_ref[...]) - lf) + _dot_sel_rhs(later_ref[...], rowtot)
            + jnp.concatenate([carry_s[...]] * pp, axis=0))
    carry_s[...] = carry_s[...] + _dot_sel_rhs(ones_ref[...], rowtot)

    def score_cols(i):
        kp = ck[i][0].reshape(page_rows, FOX_HEAD_DIM).astype(BF16)
        s = _dot_nt(qb, kp)
        return [s[:, r * LANES:(r + 1) * LANES]
                + (jnp.broadcast_to(bias[i * lf_rows + r:i * lf_rows + r + 1, :], (nrow, LANES)) + hm_ref[...])
                for r in range(lf_rows)]

    logits = jnp.concatenate([c for i in range(pp) for c in score_cols(i)], axis=1) + cn

    def values(p):
        pb = p.astype(BF16)
        out = None
        for i in range(pp):
            vb = cv[i][0].reshape(page_rows, FOX_HEAD_DIM).astype(BF16)
            part = _dot(pb[:, i * page_rows:(i + 1) * page_rows], vb)
            out = part if out is None else out + part
        return out

    flash(logits, values)

    @pl.when(p_id == pl.num_programs(1) - 1)
    def _():
        rowi = lax.broadcasted_iota(I32, (nrow, 1), 0)
        for sp in range(ds):
            kn = jnp.concatenate([kn_ref[0, sp]] * ds, axis=0)
            vn = jnp.concatenate([vn_ref[0, sp]] * ds, axis=0)
            s = jnp.sum(qf * kn, axis=1, keepdims=True)
            cn_key = jnp.concatenate([cn_s[sp * nh:(sp + 1) * nh, :]] * ds, axis=0)
            logits = jnp.where(rowi >= sp * nh, s + cn - cn_key, NEG_INF)
            flash(logits, lambda p: p * vn)
        o_ref[0] = acc_s[...] / l_s[...]


def _fox_sample(fq, fk, fv, lf, cache_k, cache_v, cache_logf, page_table, pp):
    db, ds, _ = fq.shape
    n_pool = cache_k.shape[0]
    n_pages = page_table.shape[1]
    nh = N_FOX_HEADS
    nrow = ds * nh
    lf_rows = PAGE_SIZE * nh // LANES
    cl = cache_logf.reshape(n_pool, lf_rows, LANES)
    q = fq.reshape(db, nrow, FOX_HEAD_DIM)
    kn = fk.reshape(db, ds, nh, FOX_HEAD_DIM)
    vn = fv.reshape(db, ds, nh, FOX_HEAD_DIM)
    lfn = lf.reshape(db, nrow, 1)
    hm = jnp.asarray(np.where(np.arange(LANES)[None, :] % nh == np.arange(nrow)[:, None] % nh, 0.0, NEG_INF)
                     .astype(np.float32))
    ln = np.arange(LANES)
    same_np = (ln[:, None] % nh == ln[None, :] % nh)
    same = jnp.asarray(same_np.astype(np.float32), BF16)
    suf = jnp.asarray((same_np & (ln[:, None] >= ln[None, :])).astype(np.float32), BF16)
    pg, rw = np.arange(pp * lf_rows) // lf_rows, np.arange(pp * lf_rows) % lf_rows
    later_np = (pg[None, :] < pg[:, None]) | ((pg[None, :] == pg[:, None]) & (rw[None, :] > rw[:, None]))
    later = jnp.asarray(later_np.astype(np.float32), BF16)
    ones = jnp.ones((lf_rows, pp * lf_rows), BF16)
    const2 = lambda a: pl.BlockSpec(a.shape, lambda b, p, pt: (0, 0))

    def page_spec(i, shape):
        nd = len(shape) - 1
        return pl.BlockSpec(shape, lambda b, p, pt: (pt[b, n_pages - 1 - (p * pp + i)],) + (0,) * nd)

    def per_b(shape):
        nd = len(shape) - 1
        return pl.BlockSpec(shape, lambda b, p, pt: (b,) + (0,) * nd)

    kv_block = (1, PAGE_SIZE, nh, FOX_HEAD_DIM)
    in_specs = ([page_spec(i, kv_block) for i in range(pp)]
                + [page_spec(i, kv_block) for i in range(pp)]
                + [page_spec(i, (1, lf_rows, LANES)) for i in range(pp)]
                + [per_b((1, nrow, FOX_HEAD_DIM)), per_b((1, ds, nh, FOX_HEAD_DIM)),
                   per_b((1, ds, nh, FOX_HEAD_DIM)), per_b((1, nrow, 1)),
                   const2(hm), const2(suf), const2(same), const2(later), const2(ones)])
    o = pl.pallas_call(
        functools.partial(_fox_sample_kernel, pp=pp, ds=ds),
        grid_spec=pltpu.PrefetchScalarGridSpec(
            num_scalar_prefetch=1,
            grid=(db, n_pages // pp),
            in_specs=in_specs,
            out_specs=per_b((1, nrow, FOX_HEAD_DIM)),
            scratch_shapes=[pltpu.VMEM((nrow, 1), F32), pltpu.VMEM((nrow, 1), F32),
                            pltpu.VMEM((nrow, FOX_HEAD_DIM), F32), pltpu.VMEM((lf_rows, LANES), F32),
                            pltpu.VMEM((nrow, 1), F32)]),
        out_shape=jax.ShapeDtypeStruct((db, nrow, FOX_HEAD_DIM), F32),
        compiler_params=_cparams(("arbitrary", "arbitrary")),
        name="fox_sample",
    )(page_table, *([cache_k] * pp), *([cache_v] * pp), *([cl] * pp), q, kn, vn, lfn, hm, suf, same, later, ones)
    return o.reshape(db, ds, FOX_WIDTH)


def _topk_rows(s, k, keys=None):
    if keys is None:
        keys = lax.broadcasted_iota(I32, s.shape, 0).astype(F32)
    vals, idxs = [], []
    for _ in range(k):
        m = jnp.max(s, axis=0, keepdims=True)
        i = jnp.min(jnp.where(s == m, keys, 1e9), axis=0, keepdims=True)
        s = jnp.where(keys == i, -jnp.inf, s)
        vals.append(m)
        idxs.append(i)
    return jnp.concatenate(vals, axis=0), jnp.concatenate(idxs, axis=0).astype(I32)


def _cand_slabs():
    kk = PEER_TOPK
    slabs = [((0, 1), (0, 8)), ((0, 1), (8, 16)), ((1, 2), (0, 8)), ((2, 3), (0, 8)), ((3, 4), (0, 8)),
             ((4, 5), (0, 8)), ((8, 16), (0, 1)), ((0, 8), (1, 2)), ((0, 8), (0, 1))]
    keys, mask, seen = [], [], set()
    for (a0, a1), (b0, b1) in slabs:
        for a in range(a0, a1):
            for b in range(b0, b1):
                keys.append(a * kk + b)
                mask.append(0.0 if (a, b) not in seen else -np.inf)
                seen.add((a, b))
    assert all((a, b) in seen for a in range(kk) for b in range(kk) if (a + 1) * (b + 1) <= kk)
    return slabs, np.asarray(keys, np.float32), np.asarray(mask, np.float32)


def _route_kernel(x_ref, o_ref, g_ref, wo_ref, nf_ref, wq_ref, kbt_ref, ckey_ref, cmask_ref,
                  h_ref, xn_ref, off_ref, par_ref, gate_ref):
    kk = PEER_TOPK
    idx_rows, gate_rows = [], []
    slabs = _cand_slabs()[0]
    h = x_ref[...] + _dot(o_ref[...], wo_ref[0:FOX_WIDTH, :]) + _dot(g_ref[...], wo_ref[FOX_WIDTH:, :])
    h_ref[...] = h
    xn = (h * lax.rsqrt(jnp.mean(h * h, axis=-1, keepdims=True) + RMS_EPS)) * nf_ref[...]
    xn_ref[...] = xn
    q = _dot(xn.astype(BF16), wq_ref[...]).astype(BF16)
    st = _dot_nt(kbt_ref[...], q)
    tm = st.shape[1]
    for hd in range(PEER_HEADS):
        base = hd * 2 * PEER_N_KEYS
        ts0, ti0 = _topk_rows(st[base:base + PEER_N_KEYS, :], kk)
        ts1, ti1 = _topk_rows(st[base + PEER_N_KEYS:base + 2 * PEER_N_KEYS, :], kk)
        cand = jnp.concatenate([ts0[a0:a1, :] + ts1[b0:b1, :] for (a0, a1), (b0, b1) in slabs], axis=0)
        score, sel = _topk_rows(cand + cmask_ref[...], kk, ckey_ref[...])
        sa = sel >> 4
        sb = sel & (kk - 1)
        i1 = jnp.zeros((kk, tm), I32)
        i2 = jnp.zeros((kk, tm), I32)
        for a in range(kk):
            i1 = jnp.where(sa == a, ti0[a:a + 1, :], i1)
            i2 = jnp.where(sb == a, ti1[a:a + 1, :], i2)
        e = jnp.exp(score - jnp.max(score, axis=0, keepdims=True))
        gate = e / jnp.sum(e, axis=0, keepdims=True)
        idx_rows.append(i1 * PEER_N_KEYS + i2)
        gate_rows.append(gate)
    idx = jnp.concatenate(idx_rows, axis=0).T
    off_ref[...] = (idx >> 1) << 3
    par_ref[...] = idx & 1
    gate_ref[...] = jnp.concatenate(gate_rows, axis=0).T


def _route(x, o, g, w_out, norm_ffn, peer_wq, peer_sub_keys, tm):
    t = x.shape[0]
    nb = PEER_HEADS * 2
    eye = jnp.eye(nb, dtype=F32)
    kbt = (peer_sub_keys.reshape(nb, PEER_N_KEYS, 1, PEER_HALF) * eye[:, None, :, None]).reshape(
        nb * PEER_N_KEYS, nb * PEER_HALF).astype(BF16)
    _, keys, mask = _cand_slabs()
    ckey = jnp.asarray(np.broadcast_to(keys[:, None], (keys.size, tm)))
    cmask = jnp.asarray(np.broadcast_to(mask[:, None], (mask.size, tm)))
    tok = lambda width: pl.BlockSpec((tm, width), lambda i: (i, 0))
    slot = tok(PEER_SLOTS)
    return pl.pallas_call(
        _route_kernel,
        grid=(t // tm,),
        in_specs=[tok(D_MODEL), tok(FOX_WIDTH), tok(GLA_V_WIDTH), _full((D_MODEL, D_MODEL)), _full((1, D_MODEL)),
                  _full((D_MODEL, D_MODEL)), _full(kbt.shape), _full(ckey.shape), _full(cmask.shape)],
        out_specs=[tok(D_MODEL), tok(D_MODEL), slot, slot, slot],
        out_shape=[jax.ShapeDtypeStruct((t, D_MODEL), F32), jax.ShapeDtypeStruct((t, D_MODEL), F32),
                   jax.ShapeDtypeStruct((t, PEER_SLOTS), I32), jax.ShapeDtypeStruct((t, PEER_SLOTS), I32),
                   jax.ShapeDtypeStruct((t, PEER_SLOTS), F32)],
        compiler_params=_cparams(("arbitrary",)),
        name="route",
    )(x, o, g, w_out.astype(BF16), norm_ffn[None], peer_wq.astype(BF16), kbt, ckey, cmask)


def _pack_table(tab):
    e = tab.shape[0]
    b = lax.bitcast_convert_type(tab.astype(BF16), jnp.uint16).astype(jnp.uint32)
    w = b[:, :D_MODEL // 2] | (b[:, D_MODEL // 2:] << 16)
    return lax.bitcast_convert_type(w, I32).reshape(e * 4, LANES)


def _unpack(tile):
    lo = lax.bitcast_convert_type(tile << 16, F32)
    hi = lax.bitcast_convert_type(tile & jnp.int32(-65536), F32)
    return lo, hi


def _gelu(x):
    return 0.5 * x * (1.0 + lax.erf(x * (2.0 ** -0.5)))


PEER_TOKEN_UNROLL = 8
PEER_U_CHAINS = 8
TILE_ROWS_BF16 = 16


def _peer_u_kernel(off_ref, x_ref, par_ref, gate_ref, tab_ref, w0_ref, w1_ref):
    tb = x_ref.shape[0]
    sub = lax.broadcasted_iota(I32, (SUBLANES, LANES), 0)
    lane = lax.broadcasted_iota(I32, (SUBLANES, LANES), 1)
    low = sub < 4

    def token(t, _):
        x8 = x_ref[t]
        xr = pltpu.roll(x8, 4, 0)
        xlo = jnp.where(low, x8, xr)
        xhi = jnp.where(low, xr, x8)
        parts = [jnp.zeros((SUBLANES, LANES), F32) for _ in range(PEER_U_CHAINS)]
        for j in range(PEER_SLOTS):
            off = pl.multiple_of(off_ref[t, j], SUBLANES)
            lo, hi = _unpack(tab_ref[pl.ds(off, SUBLANES), :])
            r = jnp.sum(lo * xlo + hi * xhi, axis=1, keepdims=True)
            parts[j % PEER_U_CHAINS] = jnp.where(lane == j, r, parts[j % PEER_U_CHAINS])
        r8 = parts[0]
        for p in parts[1:]:
            r8 = r8 + p
        a0 = r8[0:1] + r8[1:2] + r8[2:3] + r8[3:4]
        a1 = r8[4:5] + r8[5:6] + r8[6:7] + r8[7:8]
        odd = par_ref[pl.ds(t, 1), :] == 1
        w = gate_ref[pl.ds(t, 1), :] * _gelu(jnp.where(odd, a1, a0))
        w0_ref[pl.ds(t, 1), :] = jnp.where(odd, 0.0, w)
        w1_ref[pl.ds(t, 1), :] = jnp.where(odd, w, 0.0)
        return 0

    lax.fori_loop(0, tb, token, 0, unroll=PEER_TOKEN_UNROLL)


def _peer_v_kernel(off_ref, w0_ref, w1_ref, h_ref, e0_ref, e1_ref, m8_ref, tab_ref, y_ref, wexp):
    tb = h_ref.shape[0]
    wexp[...] = _dot_sel_lhs(w0_ref[...], e0_ref[...]) + _dot_sel_lhs(w1_ref[...], e1_ref[...])

    def token(t, _):
        wm = wexp[pl.ds(t, 1), :] * m8_ref[...]
        hi = wm.astype(BF16)
        lo = (wm - hi.astype(F32)).astype(BF16)
        tiles = []
        for j in range(PEER_SLOTS):
            off = pl.multiple_of(off_ref[t, j], SUBLANES)
            tiles.append(pltpu.bitcast(tab_ref[pl.ds(off, SUBLANES), :], BF16))
        r = _dot(jnp.concatenate([hi, lo], axis=0), jnp.concatenate(tiles, axis=0))
        y_ref[t] = h_ref[t] + (r[0:SUBLANES] + r[SUBLANES:])
        return 0

    lax.fori_loop(0, tb, token, 0, unroll=PEER_TOKEN_UNROLL)


def _peer_v_consts():
    rho = np.arange(TILE_ROWS_BF16)
    k = PEER_SLOTS * TILE_ROWS_BF16
    slot = np.arange(k) // TILE_ROWS_BF16
    r_k = np.arange(k) % TILE_ROWS_BF16
    sel = (np.arange(PEER_SLOTS)[:, None] == slot[None, :])
    e0 = (sel & (r_k[None, :] < SUBLANES)).astype(np.float32)
    e1 = (sel & (r_k[None, :] >= SUBLANES)).astype(np.float32)
    chunk = (rho // 2) % 4
    half = rho % 2
    out_row = half * 4 + chunk
    m8 = (np.arange(SUBLANES)[:, None] == out_row[r_k][None, :]).astype(np.float32)
    return jnp.asarray(e0, BF16), jnp.asarray(e1, BF16), jnp.asarray(m8)


def _peer_gather(h, xn, off, par, gate, tab_u, tab_v, tb):
    t = h.shape[0]
    x8 = xn.reshape(t, SUBLANES, LANES)
    h8 = h.reshape(t, SUBLANES, LANES)
    e0, e1, m8 = _peer_v_consts()
    smem = pl.BlockSpec((tb, PEER_SLOTS), lambda i: (i, 0), memory_space=pltpu.SMEM)
    tok3 = pl.BlockSpec((tb, SUBLANES, LANES), lambda i: (i, 0, 0))
    tok2 = pl.BlockSpec((tb, PEER_SLOTS), lambda i: (i, 0))
    tabs = pl.BlockSpec(tab_u.shape, lambda i: (0, 0), pipeline_mode=pl.Buffered(1))
    w0, w1 = pl.pallas_call(
        _peer_u_kernel,
        grid=(t // tb,),
        in_specs=[smem, tok3, tok2, tok2, tabs],
        out_specs=[tok2, tok2],
        out_shape=[jax.ShapeDtypeStruct((t, PEER_SLOTS), F32)] * 2,
        compiler_params=_cparams(("arbitrary",)),
        name="peer_u",
    )(off, x8, par, gate, tab_u)
    y8 = pl.pallas_call(
        _peer_v_kernel,
        grid=(t // tb,),
        in_specs=[smem, tok2, tok2, tok3, _full(e0.shape), _full(e1.shape), _full(m8.shape), tabs],
        out_specs=tok3,
        out_shape=jax.ShapeDtypeStruct((t, SUBLANES, LANES), F32),
        scratch_shapes=[pltpu.VMEM((tb, PEER_SLOTS * TILE_ROWS_BF16), F32)],
        compiler_params=_cparams(("arbitrary",)),
        name="peer_v",
    )(off, w0, w1, h8, e0, e1, m8, tab_v)
    return y8.reshape(t, D_MODEL)


def _pick(n, pref):
    t = min(n, pref)
    while n % t:
        t //= 2
    return t


def kernel(x_prompt, x_sample, cache_k, cache_v, cache_logf, state_gla, page_table, norm_mix, w_in, b_forget,
           q_norm, k_norm, w_gate_up, b_gate, gla_norm, w_out, norm_ffn, peer_wq, peer_sub_keys, peer_u, peer_v):
    depth = w_in.shape[0]
    assert depth == 1, "single-layer step"
    l = 0
    b, s, _ = x_prompt.shape
    db, ds, _ = x_sample.shape
    tab_u = _pack_table(peer_u[l])
    tab_v = _pack_table(peer_v[l])

    wts_p = _proj_weights(w_in[l], b_forget[l], q_norm[l], k_norm[l], w_gate_up[l], b_gate[l], prompt=True)
    (qaug, kaug, fk_p, fv_p, fvb, lf_p, gq, gk, gvb, la, gr) = _proj_prompt(x_prompt, norm_mix[l], wts_p, _pick(s, 256))
    fo_p = _fox_prompt(qaug, kaug, fvb, _pick(s, 512))
    g_p, sn_p = _gla_prompt(gq, gk, la, gvb, gr, gla_norm[l], _pick(s, 256))
    tp = b * s
    h_p, xn_p, *slots_p = _route(x_prompt.reshape(tp, D_MODEL), fo_p.reshape(tp, FOX_WIDTH),
                                 g_p.reshape(tp, GLA_V_WIDTH), w_out[l], norm_ffn[l], peer_wq[l],
                                 peer_sub_keys[l], _pick(tp, 256))
    y_p = _peer_gather(h_p, xn_p, *slots_p, tab_u, tab_v, _pick(tp, 64)).reshape(b, s, D_MODEL)

    ts = db * ds
    wts_s = _proj_weights(w_in[l], b_forget[l], q_norm[l], k_norm[l], w_gate_up[l], b_gate[l], prompt=False)
    (fq_s, fk_s, fv_s, lf_s, gq_s, gk_s, gvb_s, la_s, gr_s) = _proj_sample(
        x_sample.reshape(1, ts, D_MODEL), norm_mix[l], wts_s, _pick(ts, 256))
    r3 = lambda a: a.reshape(db, ds, a.shape[-1])
    lf_s8 = r3(lf_s)[:, :, :N_FOX_HEADS]
    n_pages = page_table.shape[1]
    fo_s = _fox_sample(r3(fq_s), r3(fk_s), r3(fv_s), lf_s8, cache_k[l], cache_v[l], cache_logf[l], page_table,
                       _pick(n_pages, 8))
    g_s, sn_s = _gla_sample(r3(gq_s), r3(gk_s), r3(la_s), r3(gvb_s), r3(gr_s), gla_norm[l], state_gla[l])
    h_s, xn_s, *slots_s = _route(x_sample.reshape(ts, D_MODEL), fo_s.reshape(ts, FOX_WIDTH).astype(BF16),
                                 g_s.reshape(ts, GLA_V_WIDTH).astype(BF16), w_out[l], norm_ffn[l],
                                 peer_wq[l], peer_sub_keys[l], _pick(ts, 256))
    y_s = _peer_gather(h_s, xn_s, *slots_s, tab_u, tab_v, _pick(ts, 64)).reshape(db, ds, D_MODEL)

    hd = (N_FOX_HEADS, FOX_HEAD_DIM)
    return (y_p, y_s,
            fk_p.reshape(1, b, s, *hd), fv_p.reshape(1, b, s, *hd), lf_p[None, :, :, :N_FOX_HEADS], sn_p[None],
            r3(fk_s).reshape(1, db, ds, *hd), r3(fv_s).reshape(1, db, ds, *hd), lf_s8[None], sn_s[None])
```

```python
import functools

import numpy as np
import jax
import jax.numpy as jnp
from jax import lax
from jax.experimental import pallas as pl
from jax.experimental.pallas import tpu as pltpu

F32 = jnp.float32
BF16 = jnp.bfloat16
I32 = jnp.int32

D_MODEL = 1024
N_FOX_HEADS = 8
FOX_HEAD_DIM = 64
N_GLA_HEADS = 4
GLA_DK = 64
GLA_DV = 128
GLA_GATE_RANK = 16
GLA_GATE_TEMP = 16.0
GLA_CHUNK = 64
PEER_HEADS = 8
PEER_N_KEYS = 128
PEER_TOPK = 16
PEER_HALF = 64
PAGE_SIZE = 128
RMS_EPS = 1e-6
NEG_INF = -1e30

FOX_WIDTH = N_FOX_HEADS * FOX_HEAD_DIM
GLA_QK_WIDTH = N_GLA_HEADS * GLA_DK
GLA_V_WIDTH = N_GLA_HEADS * GLA_DV
FOX_SCALE = FOX_HEAD_DIM ** -0.5
GLA_SCALE = GLA_DK ** -0.5
LOG2E = 1.4426950408889634
IN_SIZES = [FOX_WIDTH, FOX_WIDTH, FOX_WIDTH, N_FOX_HEADS,
            GLA_QK_WIDTH, GLA_QK_WIDTH, GLA_V_WIDTH, GLA_GATE_RANK, GLA_V_WIDTH]
IN_OFFS = [int(c) for c in np.cumsum([0] + IN_SIZES)]
PEER_SLOTS = PEER_HEADS * PEER_TOPK

LANES = 128
SUBLANES = 8
VMEM_LIMIT = 56 * 1024 * 1024


def _cparams(sem, vmem=VMEM_LIMIT):
    return pltpu.CompilerParams(dimension_semantics=sem, vmem_limit_bytes=vmem)


def _dot(a, b):
    return jnp.dot(a, b, preferred_element_type=F32)


def _dot_nt(a, b):
    return lax.dot_general(a, b, (((1,), (1,)), ((), ())), preferred_element_type=F32)


def _split3(x):
    hi = x.astype(BF16)
    r = x - hi.astype(F32)
    mid = r.astype(BF16)
    lo = (r - mid.astype(F32)).astype(BF16)
    return hi, mid, lo


def _dot_sel_rhs(sel_bf16, x):
    hi, mid, lo = _split3(x)
    return _dot(sel_bf16, hi) + _dot(sel_bf16, mid) + _dot(sel_bf16, lo)


def _dot_sel_lhs(x, sel_bf16):
    hi, mid, lo = _split3(x)
    return _dot(hi, sel_bf16) + _dot(mid, sel_bf16) + _dot(lo, sel_bf16)


def _log_sigmoid(x):
    return -(jnp.maximum(-x, 0.0) + jnp.log1p(jnp.exp(-jnp.abs(x))))


def _group_rms(raw, a_ref):
    sq = raw * raw
    hi = sq.astype(BF16)
    lo = (sq - hi.astype(F32)).astype(BF16)
    return _dot(hi, a_ref[...]) + _dot(lo, a_ref[...])


P_QA, P_KA, P_FK, P_FV, P_GQ, P_GK, P_GV, P_GR, P_SM = 0, 1024, 2048, 2560, 3072, 3328, 3584, 4096, 4608
P_NC = 4736
S_FQ, S_FK, S_FV, S_GQ, S_GK, S_GV, S_GR, S_SM = 0, 512, 1024, 1536, 1792, 2048, 2560, 3072
S_NC = 3200


def _proj_common(xn, w_ref, o_fk, o_fv, o_gq, o_gk, o_gv, o_gr, sm, a_ref, knc_ref, wg_ref, bg_ref,
                 fk_ref, fv_ref, gq_ref, gk_ref, gvb_ref, la_ref, gr_ref):
    fk_raw = _dot(xn, w_ref[:, o_fk:o_fk + 512])
    fk_ref[0] = (fk_raw * lax.rsqrt(_group_rms(fk_raw, a_ref) + RMS_EPS)) * knc_ref[...]
    fv = _dot(xn, w_ref[:, o_fv:o_fv + 512])
    fv_ref[0] = fv
    gq_ref[0] = _dot(xn, w_ref[:, o_gq:o_gq + 256]) * GLA_SCALE
    gk_ref[0] = _dot(xn, w_ref[:, o_gk:o_gk + 256])
    gvb_ref[0] = _dot(xn, w_ref[:, o_gv:o_gv + 512]).astype(BF16)
    gr_ref[0] = _dot(xn, w_ref[:, o_gr:o_gr + 512])
    la_ref[0] = _log_sigmoid(_dot(sm.astype(BF16), wg_ref[...]) + bg_ref[...]) * (1.0 / GLA_GATE_TEMP)
    return fv


def _xnorm(x_ref, nm_ref):
    x = x_ref[0]
    ms = jnp.mean(x * x, axis=-1, keepdims=True)
    return ((x * lax.rsqrt(ms + RMS_EPS)) * nm_ref[...]).astype(BF16)


def _proj_prompt_kernel(x_ref, nm_ref, w_ref, bf_ref, qn_ref, kn_ref, knc_ref, a_ref, wg_ref, bg_ref, tri_ref,
                        qaug_ref, kaug_ref, fk_ref, fv_ref, fvb_ref, lf_ref, gq_ref, gk_ref, gvb_ref, la_ref,
                        gr_ref, carry):
    tm = x_ref.shape[1]

    @pl.when(pl.program_id(1) == 0)
    def _():
        carry[...] = jnp.zeros_like(carry)

    xn = _xnorm(x_ref, nm_ref)
    sm = _dot(xn, w_ref[:, P_SM:P_SM + LANES])
    lf = _log_sigmoid(sm + bf_ref[...])
    lf_ref[0] = lf
    cum = _dot_sel_rhs(tri_ref[...], lf) + carry[...]
    carry[...] = cum[tm - 1:tm, :]

    fv = _proj_common(xn, w_ref, P_FK, P_FV, P_GQ, P_GK, P_GV, P_GR, sm, a_ref, knc_ref, wg_ref, bg_ref,
                      fk_ref, fv_ref, gq_ref, gk_ref, gvb_ref, la_ref, gr_ref)
    fvb_ref[0] = fv.astype(BF16)

    qa_all = _dot(xn, w_ref[:, P_QA:P_QA + 1024])
    ka_all = _dot(xn, w_ref[:, P_KA:P_KA + 1024])
    lane = lax.broadcasted_iota(I32, (tm, LANES), 1)
    inv = 1.0 / FOX_HEAD_DIM
    for h in range(N_FOX_HEADS):
        sl = slice(h * LANES, (h + 1) * LANES)
        qa = qa_all[:, sl]
        qn = (qa * lax.rsqrt(jnp.sum(qa * qa, axis=-1, keepdims=True) * inv + RMS_EPS)) * qn_ref[...]
        qn = jnp.where(lane >= 64, jnp.where(lane < 67, -1.0, 0.0), qn)
        qaug_ref[0, :, sl] = qn.astype(BF16)
        ka = ka_all[:, sl]
        kn = (ka * lax.rsqrt(jnp.sum(ka * ka, axis=-1, keepdims=True) * inv + RMS_EPS)) * kn_ref[...]
        c = cum[:, h:h + 1] * LOG2E
        c_hi = c.astype(BF16).astype(F32)
        r = c - c_hi
        c_mid = r.astype(BF16).astype(F32)
        c_lo = r - c_mid
        kn = jnp.where(lane == 64, c_hi, jnp.where(lane == 65, c_mid, jnp.where(lane == 66, c_lo, kn)))
        kaug_ref[0, :, sl] = kn.astype(BF16)


def _proj_sample_kernel(x_ref, nm_ref, w_ref, bf_ref, qnc_ref, knc_ref, a_ref, wg_ref, bg_ref,
                        fq_ref, fk_ref, fv_ref, lf_ref, gq_ref, gk_ref, gvb_ref, la_ref, gr_ref):
    xn = _xnorm(x_ref, nm_ref)
    sm = _dot(xn, w_ref[:, S_SM:S_SM + LANES])
    lf_ref[0] = _log_sigmoid(sm + bf_ref[...])
    fq_raw = _dot(xn, w_ref[:, S_FQ:S_FQ + 512])
    fq_ref[0] = (fq_raw * lax.rsqrt(_group_rms(fq_raw, a_ref) + RMS_EPS)) * qnc_ref[...]
    _proj_common(xn, w_ref, S_FK, S_FV, S_GQ, S_GK, S_GV, S_GR, sm, a_ref, knc_ref, wg_ref, bg_ref,
                 fk_ref, fv_ref, gq_ref, gk_ref, gvb_ref, la_ref, gr_ref)


def _full(shape):
    nd = len(shape)
    return pl.BlockSpec(shape, lambda *_: (0,) * nd)


def _proj_weights(w_in, b_forget, q_norm, k_norm, w_gate_up, b_gate, prompt):
    o = IN_OFFS
    wq, wk, wv, wf = (w_in[:, o[i]:o[i + 1]] for i in range(4))
    wgq, wgk, wgv, wga, wgr = (w_in[:, o[i]:o[i + 1]] for i in range(4, 9))
    small = jnp.concatenate([wf, wga, jnp.zeros((D_MODEL, LANES - 24), F32)], axis=1)

    def expand(w):
        w3 = w.reshape(D_MODEL, N_FOX_HEADS, FOX_HEAD_DIM)
        return jnp.concatenate([w3, jnp.zeros_like(w3)], axis=-1).reshape(D_MODEL, N_FOX_HEADS * LANES)

    if prompt:
        cols = [expand(wq), expand(wk), wk, wv, wgq, wgk, wgv, wgr, small]
    else:
        cols = [wq, wk, wv, wgq, wgk, wgv, wgr, small]
    w = jnp.concatenate(cols, axis=1).astype(BF16)
    bf = jnp.concatenate([b_forget, jnp.zeros((LANES - N_FOX_HEADS,), F32)])[None]
    zpad = jnp.zeros((LANES - FOX_HEAD_DIM,), F32)
    qn_pad = jnp.concatenate([q_norm * (FOX_SCALE * LOG2E), zpad])[None]
    kn_pad = jnp.concatenate([k_norm, zpad])[None]
    qnc = jnp.tile(q_norm * FOX_SCALE, N_FOX_HEADS)[None]
    knc = jnp.tile(k_norm, N_FOX_HEADS)[None]
    g = np.arange(FOX_WIDTH) // FOX_HEAD_DIM
    a = jnp.asarray((g[:, None] == g[None, :]).astype(np.float32) / FOX_HEAD_DIM, BF16)
    wg = jnp.zeros((LANES, GLA_QK_WIDTH), F32).at[N_FOX_HEADS:N_FOX_HEADS + GLA_GATE_RANK].set(w_gate_up).astype(BF16)
    return w, bf, qn_pad, kn_pad, qnc, knc, a, wg, b_gate[None]


def _proj_prompt(x, norm_mix, wts, tm):
    b, s, _ = x.shape
    w, bf, qn_pad, kn_pad, _, knc, a, wg, bg = wts
    tri = jnp.asarray(np.tril(np.ones((tm, tm), np.float32)), BF16)
    tok = lambda width: pl.BlockSpec((1, tm, width), lambda i, j: (i, j, 0))
    shp = lambda width, dt: jax.ShapeDtypeStruct((b, s, width), dt)
    return pl.pallas_call(
        _proj_prompt_kernel,
        grid=(b, s // tm),
        in_specs=[tok(D_MODEL), _full((1, D_MODEL)), _full(w.shape), _full((1, LANES)), _full((1, LANES)),
                  _full((1, LANES)), _full((1, FOX_WIDTH)), _full(a.shape), _full(wg.shape),
                  _full((1, GLA_QK_WIDTH)), _full((tm, tm))],
        out_specs=[tok(1024), tok(1024), tok(512), tok(512), tok(512), tok(LANES), tok(256), tok(256), tok(512),
                   tok(256), tok(512)],
        out_shape=[shp(1024, BF16), shp(1024, BF16), shp(512, F32), shp(512, F32), shp(512, BF16), shp(LANES, F32),
                   shp(256, F32), shp(256, F32), shp(512, BF16), shp(256, F32), shp(512, F32)],
        scratch_shapes=[pltpu.VMEM((1, LANES), F32)],
        compiler_params=_cparams(("arbitrary", "arbitrary")),
        name="proj_prompt",
    )(x, norm_mix[None], w, bf, qn_pad, kn_pad, knc, a, wg, bg, tri)


def _proj_sample(x, norm_mix, wts, tm):
    b, s, _ = x.shape
    w, bf, _, _, qnc, knc, a, wg, bg = wts
    tok = lambda width: pl.BlockSpec((1, tm, width), lambda i, j: (i, j, 0))
    shp = lambda width, dt: jax.ShapeDtypeStruct((b, s, width), dt)
    return pl.pallas_call(
        _proj_sample_kernel,
        grid=(b, s // tm),
        in_specs=[tok(D_MODEL), _full((1, D_MODEL)), _full(w.shape), _full((1, LANES)), _full((1, FOX_WIDTH)),
                  _full((1, FOX_WIDTH)), _full(a.shape), _full(wg.shape), _full((1, GLA_QK_WIDTH))],
        out_specs=[tok(512), tok(512), tok(512), tok(LANES), tok(256), tok(256), tok(512), tok(256), tok(512)],
        out_shape=[shp(512, F32), shp(512, F32), shp(512, F32), shp(LANES, F32), shp(256, F32), shp(256, F32),
                   shp(512, BF16), shp(256, F32), shp(512, F32)],
        compiler_params=_cparams(("arbitrary", "arbitrary")),
        name="proj_sample",
    )(x, norm_mix[None], w, bf, qnc, knc, a, wg, bg)


FOX_HEADS_PER_STEP = 4


def _fox_prompt_kernel(kaug_ref, qaug_ref, vt_ref, o_ref):
    tq = qaug_ref.shape[1]
    tk = vt_ref.shape[4]
    nh = vt_ref.shape[1]
    qi = pl.program_id(2)
    qs = [qaug_ref[0, :, g * LANES:(g + 1) * LANES] for g in range(nh)]

    def scores(kj, g):
        k = kaug_ref[0, pl.ds(pl.multiple_of(kj * tk, tk), tk), g * LANES:(g + 1) * LANES]
        return _dot_nt(k, qs[g])

    def softmax_step(st, carry):
        m, l, acc = carry
        m_new = jnp.maximum(m, jnp.max(st, axis=0, keepdims=True))
        alpha = jnp.exp2(m - m_new)
        p = jnp.exp2(st - m_new)
        return m_new, alpha * l + jnp.sum(p, axis=0, keepdims=True), alpha * acc, p.astype(BF16)

    def block(kj, carry, mask):
        sts = [scores(kj, g) for g in range(nh)]
        if mask is not None:
            sts = [jnp.where(mask, st, NEG_INF) for st in sts]
        mid = [softmax_step(sts[g], carry[g]) for g in range(nh)]
        return tuple((m, l, acc + _dot(vt_ref[0, g, kj], p)) for g, (m, l, acc, p) in enumerate(mid))

    init = tuple((jnp.full((1, tq), NEG_INF, F32), jnp.zeros((1, tq), F32), jnp.zeros((FOX_HEAD_DIM, tq), F32))
                 for _ in range(nh))
    carry = lax.fori_loop(0, qi, lambda kj, c: block(kj, c, None), init)
    row = lax.broadcasted_iota(I32, (tk, tq), 0)
    col = lax.broadcasted_iota(I32, (tk, tq), 1)
    carry = block(qi, carry, row <= col)
    for g in range(nh):
        _, l, acc = carry[g]
        o_ref[0, g] = (acc / l).astype(BF16)


def _fox_prompt(qaug, kaug, fvb, t):
    b, s, _ = qaug.shape
    n = s // t
    nh = FOX_HEADS_PER_STEP
    vt = fvb.reshape(b, n, t, N_FOX_HEADS, FOX_HEAD_DIM).transpose(0, 3, 1, 4, 2)
    ot = pl.pallas_call(
        _fox_prompt_kernel,
        grid=(b, N_FOX_HEADS // nh, n),
        in_specs=[pl.BlockSpec((1, s, nh * LANES), lambda i, h, q: (i, 0, h), pipeline_mode=pl.Buffered(1)),
                  pl.BlockSpec((1, t, nh * LANES), lambda i, h, q: (i, q, h)),
                  pl.BlockSpec((1, nh, n, FOX_HEAD_DIM, t), lambda i, h, q: (i, h, 0, 0, 0),
                               pipeline_mode=pl.Buffered(1))],
        out_specs=pl.BlockSpec((1, nh, FOX_HEAD_DIM, t), lambda i, h, q: (i, h, 0, q)),
        out_shape=jax.ShapeDtypeStruct((b, N_FOX_HEADS, FOX_HEAD_DIM, s), BF16),
        compiler_params=_cparams(("arbitrary", "arbitrary", "arbitrary")),
        name="fox_prompt",
    )(kaug, qaug, vt)
    return ot.transpose(0, 3, 1, 2).reshape(b, s, FOX_WIDTH)


def _gla_gate_out(o, gr, gn_ref):
    on = (o * lax.rsqrt(jnp.mean(o * o, axis=-1, keepdims=True) + RMS_EPS)) * gn_ref[...]
    return on * (gr * jax.nn.sigmoid(gr))


def _gla_prompt_kernel(gq_ref, gk_ref, la_ref, gkt_ref, lat_ref, gv_ref, gr_ref, gn_ref, tril_ref, triu_ref,
                       g_ref, sn_ref, state):
    c = GLA_CHUNK
    nch = gq_ref.shape[1] // c

    @pl.when(pl.program_id(1) == 0)
    def _():
        state[...] = jnp.zeros_like(state)

    lane = lax.broadcasted_iota(I32, (c, GLA_QK_WIDTH), 1)
    row = lax.broadcasted_iota(I32, (c, c), 0)
    col = lax.broadcasted_iota(I32, (c, c), 1)
    causal = col <= row
    for ci in range(nch):
        rs = slice(ci * c, (ci + 1) * c)
        q = gq_ref[0, rs, :]
        k = gk_ref[0, rs, :]
        v = gv_ref[0, rs, :]
        cum = _dot_sel_rhs(tril_ref[...], la_ref[0, rs, :])
        cum_t = _dot_sel_lhs(lat_ref[0, ci], triu_ref[...])
        last_col = cum_t[:, c - 1:c]
        mid = cum[c // 2 - 1:c // 2, :]
        qs = (q * jnp.exp(cum - mid))
        ks = (k * jnp.exp(mid - cum)).astype(BF16)
        qd = q * jnp.exp(cum)
        kh_t = (gkt_ref[0, ci] * jnp.exp(last_col - cum_t)).astype(BF16)
        st = state[...]
        st_b = st.astype(BF16)
        upd = _dot(kh_t, v)
        new_rows = []
        for h in range(N_GLA_HEADS):
            hm = (lane >= h * GLA_DK) & (lane < (h + 1) * GLA_DK)
            sc = _dot_nt(jnp.where(hm, qs, 0.0).astype(BF16), ks)
            sc = jnp.where(causal, sc, 0.0).astype(BF16)
            vs = slice(h * GLA_DV, (h + 1) * GLA_DV)
            o = _dot(sc, v[:, vs]) + _dot(jnp.where(hm, qd, 0.0).astype(BF16), st_b)
            g_ref[0, rs, vs] = _gla_gate_out(o, gr_ref[0, rs, vs], gn_ref).astype(BF16)
            new_rows.append(upd[h * GLA_DK:(h + 1) * GLA_DK, vs])
        state[...] = jnp.exp(last_col) * st + jnp.concatenate(new_rows, axis=0)

    @pl.when(pl.program_id(1) == pl.num_programs(1) - 1)
    def _():
        sn_ref[0] = state[...]


def _gla_prompt(gq, gk, la, gvb, gr, gla_norm, tg):
    b, s, _ = gq.shape
    c = GLA_CHUNK
    nch = tg // c
    to_t = lambda a: a.reshape(b, s // c, c, GLA_QK_WIDTH).transpose(0, 1, 3, 2)
    tril = jnp.asarray(np.tril(np.ones((c, c), np.float32)), BF16)
    triu = jnp.asarray(np.triu(np.ones((c, c), np.float32)), BF16)
    tok = lambda width: pl.BlockSpec((1, tg, width), lambda i, j: (i, j, 0))
    tr = pl.BlockSpec((1, nch, GLA_QK_WIDTH, c), lambda i, j: (i, j, 0, 0))
    g, sn = pl.pallas_call(
        _gla_prompt_kernel,
        grid=(b, s // tg),
        in_specs=[tok(256), tok(256), tok(256), tr, tr, tok(512), tok(512), _full((1, GLA_DV)), _full((c, c)),
                  _full((c, c))],
        out_specs=[tok(512), pl.BlockSpec((1, GLA_QK_WIDTH, GLA_DV), lambda i, j: (i, 0, 0))],
        out_shape=[jax.ShapeDtypeStruct((b, s, GLA_V_WIDTH), BF16),
                   jax.ShapeDtypeStruct((b, GLA_QK_WIDTH, GLA_DV), F32)],
        scratch_shapes=[pltpu.VMEM((GLA_QK_WIDTH, GLA_DV), F32)],
        compiler_params=_cparams(("arbitrary", "arbitrary")),
        name="gla_prompt",
    )(gq, gk, la, to_t(gk), to_t(la), gvb, gr, gla_norm[None], tril, triu)
    return g, sn.reshape(b, N_GLA_HEADS, GLA_DK, GLA_DV)


def _gla_sample_kernel(qt_ref, kt_ref, lt_ref, v_ref, gr_ref, gn_ref, s0_ref, g_ref, sn_ref):
    ds = v_ref.shape[1]
    for h in range(N_GLA_HEADS):
        st = s0_ref[0, h]
        rs = slice(h * GLA_DK, (h + 1) * GLA_DK)
        vs = slice(h * GLA_DV, (h + 1) * GLA_DV)
        for t in range(ds):
            a = jnp.exp(lt_ref[0, rs, t:t + 1])
            st = a * st + kt_ref[0, rs, t:t + 1] * v_ref[0, t:t + 1, vs].astype(F32)
            o = jnp.sum(qt_ref[0, rs, t:t + 1] * st, axis=0, keepdims=True)
            g_ref[0, t:t + 1, vs] = _gla_gate_out(o, gr_ref[0, t:t + 1, vs], gn_ref)
        sn_ref[0, h] = st


def _gla_sample(gq, gk, la, gv, gr, gla_norm, s0):
    b, ds, _ = gq.shape
    t3 = lambda a: a.transpose(0, 2, 1)
    colb = pl.BlockSpec((1, GLA_QK_WIDTH, ds), lambda i: (i, 0, 0))
    rowb = pl.BlockSpec((1, ds, GLA_V_WIDTH), lambda i: (i, 0, 0))
    stb = pl.BlockSpec((1, N_GLA_HEADS, GLA_DK, GLA_DV), lambda i: (i, 0, 0, 0))
    return pl.pallas_call(
        _gla_sample_kernel,
        grid=(b,),
        in_specs=[colb, colb, colb, rowb, rowb, _full((1, GLA_DV)), stb],
        out_specs=[rowb, stb],
        out_shape=[jax.ShapeDtypeStruct((b, ds, GLA_V_WIDTH), F32),
                   jax.ShapeDtypeStruct((b, N_GLA_HEADS, GLA_DK, GLA_DV), F32)],
        compiler_params=_cparams(("arbitrary",)),
        name="gla_sample",
    )(t3(gq), t3(gk), t3(la), gv, gr, gla_norm[None], s0)


def _fox_sample_kernel(pt_ref, *refs, pp, ds):
    del pt_ref
    ck = refs[0:pp]
    cv = refs[pp:2 * pp]
    cl = refs[2 * pp:3 * pp]
    q_ref, kn_ref, vn_ref, lfn_ref, mgt_ref, bm_ref, o_ref, m_s, l_s, acc_s, carry_s, cn_s = refs[3 * pp:]
    nh = N_FOX_HEADS
    nrow = ds * nh
    p_id = pl.program_id(1)

    @pl.when(p_id == 0)
    def _():
        m_s[...] = jnp.full_like(m_s, NEG_INF)
        l_s[...] = jnp.zeros_like(l_s)
        acc_s[...] = jnp.zeros_like(acc_s)
        carry_s[...] = jnp.zeros_like(carry_s)
        run = lfn_ref[0, 0:nh, :]
        cn_s[0:nh, :] = run
        for t in range(1, ds):
            run = run + lfn_ref[0, t * nh:(t + 1) * nh, :]
            cn_s[t * nh:(t + 1) * nh, :] = run

    qf = q_ref[0]
    qb = qf.astype(BF16)
    cn = cn_s[...]

    def flash(logits, pv_fn):
        m = m_s[...]
        m_new = jnp.maximum(m, jnp.max(logits, axis=1, keepdims=True))
        alpha = jnp.exp(m - m_new)
        p = jnp.exp(logits - m_new)
        l_s[...] = alpha * l_s[...] + jnp.sum(p, axis=1, keepdims=True)
        acc_s[...] = alpha * acc_s[...] + pv_fn(p)
        m_s[...] = m_new

    lt = jnp.concatenate([cl[i][0] for i in range(pp)], axis=0)
    excl = _dot_sel_lhs(lt, mgt_ref[...])
    tot = jnp.sum(lt, axis=1, keepdims=True)
    run = carry_s[...]
    cols = []
    for i in range(pp):
        kt = ck[i][0].reshape(FOX_WIDTH, PAGE_SIZE).astype(BF16)
        bias = excl[i * nh:(i + 1) * nh, :] + run
        run = run + tot[i * nh:(i + 1) * nh, :]
        cols.append(_dot(qb, kt) + jnp.concatenate([bias] * ds, axis=0))
    carry_s[...] = run

    def values(p):
        pb = p.astype(BF16)
        out = None
        for i in range(pp):
            vt = cv[i][0].reshape(FOX_WIDTH, PAGE_SIZE).astype(BF16)
            part = _dot_nt(pb[:, i * PAGE_SIZE:(i + 1) * PAGE_SIZE], vt)
            out = part if out is None else out + part
        return out

    flash(jnp.concatenate(cols, axis=1) + cn, values)

    @pl.when(p_id == pl.num_programs(1) - 1)
    def _():
        rowi = lax.broadcasted_iota(I32, (nrow, 1), 0)
        for sp in range(ds):
            kn = kn_ref[0, sp:sp + 1, :]
            vn = vn_ref[0, sp:sp + 1, :]
            s = jnp.sum(qf * kn, axis=1, keepdims=True)
            cn_key = jnp.concatenate([cn_s[sp * nh:(sp + 1) * nh, :]] * ds, axis=0)
            logits = jnp.where(rowi >= sp * nh, s + cn - cn_key, NEG_INF)
            flash(logits, lambda p: p * vn)
        res = (acc_s[...] / l_s[...]) * bm_ref[...]
        for t in range(ds):
            o_ref[0, t:t + 1, :] = jnp.sum(res[t * nh:(t + 1) * nh, :], axis=0, keepdims=True)


def _fox_sample(fq, fk, fv, lf, cache_k, cache_v, cache_logf, page_table, pp):
    db, ds, _ = fq.shape
    n_pages = page_table.shape[1]
    nh = N_FOX_HEADS
    nrow = ds * nh
    ckt = cache_k.transpose(0, 2, 3, 1)
    cvt = cache_v.transpose(0, 2, 3, 1)
    clt = cache_logf.transpose(0, 2, 1)
    hmask = (np.arange(FOX_WIDTH)[None, :] // FOX_HEAD_DIM == np.arange(nh)[:, None]).astype(np.float32)
    bm = jnp.asarray(np.tile(hmask, (ds, 1)))
    qbd = (fq[:, :, None, :] * jnp.asarray(hmask)[None, None]).reshape(db, nrow, FOX_WIDTH)
    lfn = lf.reshape(db, nrow, 1)
    mgt = jnp.asarray(np.tril(np.ones((PAGE_SIZE, PAGE_SIZE), np.float32), -1), BF16)

    def page_spec(i, shape):
        nd = len(shape) - 1
        return pl.BlockSpec(shape, lambda b, p, pt: (pt[b, n_pages - 1 - (p * pp + i)],) + (0,) * nd)

    per_b = lambda shape: pl.BlockSpec(shape, lambda b, p, pt: (b, 0, 0))
    const2 = lambda a: pl.BlockSpec(a.shape, lambda b, p, pt: (0, 0))
    kv_block = (1, nh, FOX_HEAD_DIM, PAGE_SIZE)
    in_specs = ([page_spec(i, kv_block) for i in range(pp)]
                + [page_spec(i, kv_block) for i in range(pp)]
                + [page_spec(i, (1, nh, PAGE_SIZE)) for i in range(pp)]
                + [per_b((1, nrow, FOX_WIDTH)), per_b((1, ds, FOX_WIDTH)), per_b((1, ds, FOX_WIDTH)),
                   per_b((1, nrow, 1)), const2(mgt), const2(bm)])
    return pl.pallas_call(
        functools.partial(_fox_sample_kernel, pp=pp, ds=ds),
        grid_spec=pltpu.PrefetchScalarGridSpec(
            num_scalar_prefetch=1,
            grid=(db, n_pages // pp),
            in_specs=in_specs,
            out_specs=per_b((1, ds, FOX_WIDTH)),
            scratch_shapes=[pltpu.VMEM((nrow, 1), F32), pltpu.VMEM((nrow, 1), F32),
                            pltpu.VMEM((nrow, FOX_WIDTH), F32), pltpu.VMEM((nh, 1), F32),
                            pltpu.VMEM((nrow, 1), F32)]),
        out_shape=jax.ShapeDtypeStruct((db, ds, FOX_WIDTH), F32),
        compiler_params=_cparams(("arbitrary", "arbitrary")),
        name="fox_sample",
    )(page_table, *([ckt] * pp), *([cvt] * pp), *([clt] * pp), qbd, fk, fv, lfn, mgt, bm)


def _topk_rows(s, k, keys=None):
    if keys is None:
        keys = lax.broadcasted_iota(I32, s.shape, 0).astype(F32)
    vals, idxs = [], []
    for _ in range(k):
        m = jnp.max(s, axis=0, keepdims=True)
        i = jnp.min(jnp.where(s == m, keys, 1e9), axis=0, keepdims=True)
        s = jnp.where(keys == i, -jnp.inf, s)
        vals.append(m)
        idxs.append(i)
    return jnp.concatenate(vals, axis=0), jnp.concatenate(idxs, axis=0).astype(I32)


def _cand_slabs():
    kk = PEER_TOPK
    slabs = [((0, 1), (0, 8)), ((0, 1), (8, 16)), ((1, 2), (0, 8)), ((2, 3), (0, 8)), ((3, 4), (0, 8)),
             ((4, 5), (0, 8)), ((8, 16), (0, 1)), ((0, 8), (1, 2)), ((0, 8), (0, 1))]
    keys, mask, seen = [], [], set()
    for (a0, a1), (b0, b1) in slabs:
        for a in range(a0, a1):
            for b in range(b0, b1):
                keys.append(a * kk + b)
                mask.append(0.0 if (a, b) not in seen else -np.inf)
                seen.add((a, b))
    assert all((a, b) in seen for a in range(kk) for b in range(kk) if (a + 1) * (b + 1) <= kk)
    return slabs, np.asarray(keys, np.float32), np.asarray(mask, np.float32)


def _route_kernel(x_ref, o_ref, g_ref, wo_ref, nf_ref, wq_ref, kbt_ref, ckey_ref, cmask_ref,
                  h_ref, xn_ref, off_ref, par_ref, gate_ref):
    kk = PEER_TOPK
    idx_rows, gate_rows = [], []
    slabs = _cand_slabs()[0]
    h = x_ref[...] + _dot(o_ref[...], wo_ref[0:FOX_WIDTH, :]) + _dot(g_ref[...], wo_ref[FOX_WIDTH:, :])
    h_ref[...] = h
    xn = (h * lax.rsqrt(jnp.mean(h * h, axis=-1, keepdims=True) + RMS_EPS)) * nf_ref[...]
    xn_ref[...] = xn
    q = _dot(xn.astype(BF16), wq_ref[...]).astype(BF16)
    st = _dot_nt(kbt_ref[...], q)
    tm = st.shape[1]
    for hd in range(PEER_HEADS):
        base = hd * 2 * PEER_N_KEYS
        ts0, ti0 = _topk_rows(st[base:base + PEER_N_KEYS, :], kk)
        ts1, ti1 = _topk_rows(st[base + PEER_N_KEYS:base + 2 * PEER_N_KEYS, :], kk)
        cand = jnp.concatenate([ts0[a0:a1, :] + ts1[b0:b1, :] for (a0, a1), (b0, b1) in slabs], axis=0)
        score, sel = _topk_rows(cand + cmask_ref[...], kk, ckey_ref[...])
        sa = sel >> 4
        sb = sel & (kk - 1)
        i1 = jnp.zeros((kk, tm), I32)
        i2 = jnp.zeros((kk, tm), I32)
        for a in range(kk):
            i1 = jnp.where(sa == a, ti0[a:a + 1, :], i1)
            i2 = jnp.where(sb == a, ti1[a:a + 1, :], i2)
        e = jnp.exp(score - jnp.max(score, axis=0, keepdims=True))
        gate = e / jnp.sum(e, axis=0, keepdims=True)
        idx_rows.append(i1 * PEER_N_KEYS + i2)
        gate_rows.append(gate)
    idx = jnp.concatenate(idx_rows, axis=0).T
    off_ref[...] = (idx >> 1) << 3
    par_ref[...] = idx & 1
    gate_ref[...] = jnp.concatenate(gate_rows, axis=0).T


def _route(x, o, g, w_out, norm_ffn, peer_wq, peer_sub_keys, tm):
    t = x.shape[0]
    nb = PEER_HEADS * 2
    eye = jnp.eye(nb, dtype=F32)
    kbt = (peer_sub_keys.reshape(nb, PEER_N_KEYS, 1, PEER_HALF) * eye[:, None, :, None]).reshape(
        nb * PEER_N_KEYS, nb * PEER_HALF).astype(BF16)
    _, keys, mask = _cand_slabs()
    ckey = jnp.asarray(np.broadcast_to(keys[:, None], (keys.size, tm)))
    cmask = jnp.asarray(np.broadcast_to(mask[:, None], (mask.size, tm)))
    tok = lambda width: pl.BlockSpec((tm, width), lambda i: (i, 0))
    slot = tok(PEER_SLOTS)
    return pl.pallas_call(
        _route_kernel,
        grid=(t // tm,),
        in_specs=[tok(D_MODEL), tok(FOX_WIDTH), tok(GLA_V_WIDTH), _full((D_MODEL, D_MODEL)), _full((1, D_MODEL)),
                  _full((D_MODEL, D_MODEL)), _full(kbt.shape), _full(ckey.shape), _full(cmask.shape)],
        out_specs=[tok(D_MODEL), tok(D_MODEL), slot, slot, slot],
        out_shape=[jax.ShapeDtypeStruct((t, D_MODEL), F32), jax.ShapeDtypeStruct((t, D_MODEL), F32),
                   jax.ShapeDtypeStruct((t, PEER_SLOTS), I32), jax.ShapeDtypeStruct((t, PEER_SLOTS), I32),
                   jax.ShapeDtypeStruct((t, PEER_SLOTS), F32)],
        compiler_params=_cparams(("arbitrary",)),
        name="route",
    )(x, o, g, w_out.astype(BF16), norm_ffn[None], peer_wq.astype(BF16), kbt, ckey, cmask)


def _pack_table(tab):
    e = tab.shape[0]
    b = lax.bitcast_convert_type(tab.astype(BF16), jnp.uint16).astype(jnp.uint32)
    w = b[:, :D_MODEL // 2] | (b[:, D_MODEL // 2:] << 16)
    return lax.bitcast_convert_type(w, I32).reshape(e * 4, LANES)


def _unpack(tile):
    lo = lax.bitcast_convert_type(tile << 16, F32)
    hi = lax.bitcast_convert_type(tile & jnp.int32(-65536), F32)
    return lo, hi


def _gelu(x):
    return 0.5 * x * (1.0 + lax.erf(x * (2.0 ** -0.5)))


PEER_TOKEN_UNROLL = 8
PEER_U_CHAINS = 8
TILE_ROWS_BF16 = 16


def _peer_u_kernel(off_ref, x_ref, par_ref, gate_ref, tab_ref, w0_ref, w1_ref):
    tb = x_ref.shape[0]
    sub = lax.broadcasted_iota(I32, (SUBLANES, LANES), 0)
    lane = lax.broadcasted_iota(I32, (SUBLANES, LANES), 1)
    low = sub < 4

    def token(t, _):
        x8 = x_ref[t]
        xr = pltpu.roll(x8, 4, 0)
        xlo = jnp.where(low, x8, xr)
        xhi = jnp.where(low, xr, x8)
        parts = [jnp.zeros((SUBLANES, LANES), F32) for _ in range(PEER_U_CHAINS)]
        for j in range(PEER_SLOTS):
            off = pl.multiple_of(off_ref[t, j], SUBLANES)
            lo, hi = _unpack(tab_ref[pl.ds(off, SUBLANES), :])
            r = jnp.sum(lo * xlo + hi * xhi, axis=1, keepdims=True)
            parts[j % PEER_U_CHAINS] = jnp.where(lane == j, r, parts[j % PEER_U_CHAINS])
        r8 = parts[0]
        for p in parts[1:]:
            r8 = r8 + p
        a0 = r8[0:1] + r8[1:2] + r8[2:3] + r8[3:4]
        a1 = r8[4:5] + r8[5:6] + r8[6:7] + r8[7:8]
        odd = par_ref[pl.ds(t, 1), :] == 1
        w = gate_ref[pl.ds(t, 1), :] * _gelu(jnp.where(odd, a1, a0))
        w0_ref[pl.ds(t, 1), :] = jnp.where(odd, 0.0, w)
        w1_ref[pl.ds(t, 1), :] = jnp.where(odd, w, 0.0)
        return 0

    lax.fori_loop(0, tb, token, 0, unroll=PEER_TOKEN_UNROLL)


def _peer_v_kernel(off_ref, w0_ref, w1_ref, h_ref, e0_ref, e1_ref, m8_ref, tab_ref, y_ref, wexp):
    tb = h_ref.shape[0]
    wexp[...] = _dot_sel_lhs(w0_ref[...], e0_ref[...]) + _dot_sel_lhs(w1_ref[...], e1_ref[...])

    def token(t, _):
        wm = wexp[pl.ds(t, 1), :] * m8_ref[...]
        hi = wm.astype(BF16)
        lo = (wm - hi.astype(F32)).astype(BF16)
        tiles = []
        for j in range(PEER_SLOTS):
            off = pl.multiple_of(off_ref[t, j], SUBLANES)
            tiles.append(pltpu.bitcast(tab_ref[pl.ds(off, SUBLANES), :], BF16))
        r = _dot(jnp.concatenate([hi, lo], axis=0), jnp.concatenate(tiles, axis=0))
        y_ref[t] = h_ref[t] + (r[0:SUBLANES] + r[SUBLANES:])
        return 0

    lax.fori_loop(0, tb, token, 0, unroll=PEER_TOKEN_UNROLL)


def _peer_v_consts():
    rho = np.arange(TILE_ROWS_BF16)
    k = PEER_SLOTS * TILE_ROWS_BF16
    slot = np.arange(k) // TILE_ROWS_BF16
    r_k = np.arange(k) % TILE_ROWS_BF16
    sel = (np.arange(PEER_SLOTS)[:, None] == slot[None, :])
    e0 = (sel & (r_k[None, :] < SUBLANES)).astype(np.float32)
    e1 = (sel & (r_k[None, :] >= SUBLANES)).astype(np.float32)
    chunk = (rho // 2) % 4
    half = rho % 2
    out_row = half * 4 + chunk
    m8 = (np.arange(SUBLANES)[:, None] == out_row[r_k][None, :]).astype(np.float32)
    return jnp.asarray(e0, BF16), jnp.asarray(e1, BF16), jnp.asarray(m8)


def _peer_gather(h, xn, off, par, gate, tab_u, tab_v, tb):
    t = h.shape[0]
    x8 = xn.reshape(t, SUBLANES, LANES)
    h8 = h.reshape(t, SUBLANES, LANES)
    e0, e1, m8 = _peer_v_consts()
    smem = pl.BlockSpec((tb, PEER_SLOTS), lambda i: (i, 0), memory_space=pltpu.SMEM)
    tok3 = pl.BlockSpec((tb, SUBLANES, LANES), lambda i: (i, 0, 0))
    tok2 = pl.BlockSpec((tb, PEER_SLOTS), lambda i: (i, 0))
    tabs = pl.BlockSpec(tab_u.shape, lambda i: (0, 0), pipeline_mode=pl.Buffered(1))
    w0, w1 = pl.pallas_call(
        _peer_u_kernel,
        grid=(t // tb,),
        in_specs=[smem, tok3, tok2, tok2, tabs],
        out_specs=[tok2, tok2],
        out_shape=[jax.ShapeDtypeStruct((t, PEER_SLOTS), F32)] * 2,
        compiler_params=_cparams(("arbitrary",)),
        name="peer_u",
    )(off, x8, par, gate, tab_u)
    y8 = pl.pallas_call(
        _peer_v_kernel,
        grid=(t // tb,),
        in_specs=[smem, tok2, tok2, tok3, _full(e0.shape), _full(e1.shape), _full(m8.shape), tabs],
        out_specs=tok3,
        out_shape=jax.ShapeDtypeStruct((t, SUBLANES, LANES), F32),
        scratch_shapes=[pltpu.VMEM((tb, PEER_SLOTS * TILE_ROWS_BF16), F32)],
        compiler_params=_cparams(("arbitrary",)),
        name="peer_v",
    )(off, w0, w1, h8, e0, e1, m8, tab_v)
    return y8.reshape(t, D_MODEL)


def _pick(n, pref):
    t = min(n, pref)
    while n % t:
        t //= 2
    return t


def kernel(x_prompt, x_sample, cache_k, cache_v, cache_logf, state_gla, page_table, norm_mix, w_in, b_forget,
           q_norm, k_norm, w_gate_up, b_gate, gla_norm, w_out, norm_ffn, peer_wq, peer_sub_keys, peer_u, peer_v):
    depth = w_in.shape[0]
    assert depth == 1, "single-layer step"
    l = 0
    b, s, _ = x_prompt.shape
    db, ds, _ = x_sample.shape
    tab_u = _pack_table(peer_u[l])
    tab_v = _pack_table(peer_v[l])

    wts_p = _proj_weights(w_in[l], b_forget[l], q_norm[l], k_norm[l], w_gate_up[l], b_gate[l], prompt=True)
    (qaug, kaug, fk_p, fv_p, fvb, lf_p, gq, gk, gvb, la, gr) = _proj_prompt(x_prompt, norm_mix[l], wts_p, _pick(s, 256))
    fo_p = _fox_prompt(qaug, kaug, fvb, _pick(s, 512))
    g_p, sn_p = _gla_prompt(gq, gk, la, gvb, gr, gla_norm[l], _pick(s, 256))
    tp = b * s
    h_p, xn_p, *slots_p = _route(x_prompt.reshape(tp, D_MODEL), fo_p.reshape(tp, FOX_WIDTH),
                                 g_p.reshape(tp, GLA_V_WIDTH), w_out[l], norm_ffn[l], peer_wq[l],
                                 peer_sub_keys[l], _pick(tp, 256))
    y_p = _peer_gather(h_p, xn_p, *slots_p, tab_u, tab_v, _pick(tp, 64)).reshape(b, s, D_MODEL)

    ts = db * ds
    wts_s = _proj_weights(w_in[l], b_forget[l], q_norm[l], k_norm[l], w_gate_up[l], b_gate[l], prompt=False)
    (fq_s, fk_s, fv_s, lf_s, gq_s, gk_s, gvb_s, la_s, gr_s) = _proj_sample(
        x_sample.reshape(1, ts, D_MODEL), norm_mix[l], wts_s, _pick(ts, 256))
    r3 = lambda a: a.reshape(db, ds, a.shape[-1])
    lf_s8 = r3(lf_s)[:, :, :N_FOX_HEADS]
    n_pages = page_table.shape[1]
    fo_s = _fox_sample(r3(fq_s), r3(fk_s), r3(fv_s), lf_s8, cache_k[l], cache_v[l], cache_logf[l], page_table,
                       _pick(n_pages, 8))
    g_s, sn_s = _gla_sample(r3(gq_s), r3(gk_s), r3(la_s), r3(gvb_s), r3(gr_s), gla_norm[l], state_gla[l])
    h_s, xn_s, *slots_s = _route(x_sample.reshape(ts, D_MODEL), fo_s.reshape(ts, FOX_WIDTH).astype(BF16),
                                 g_s.reshape(ts, GLA_V_WIDTH).astype(BF16), w_out[l], norm_ffn[l],
                                 peer_wq[l], peer_sub_keys[l], _pick(ts, 256))
    y_s = _peer_gather(h_s, xn_s, *slots_s, tab_u, tab_v, _pick(ts, 64)).reshape(db, ds, D_MODEL)

    hd = (N_FOX_HEADS, FOX_HEAD_DIM)
    return (y_p, y_s,
            fk_p.reshape(1, b, s, *hd), fv_p.reshape(1, b, s, *hd), lf_p[None, :, :, :N_FOX_HEADS], sn_p[None],
            r3(fk_s).reshape(1, db, ds, *hd), r3(fv_s).reshape(1, db, ds, *hd), lf_s8[None], sn_s[None])
```
